```python
import jax, jax.numpy as jnp
from jax import lax
import numpy as np

D_MODEL = 1024
BATCH = 4
SEQ = 4096
DEPTH = 1
DEC_BATCH = 128
DEC_SEQ = 8
PAST_LEN = 2048
PAGE_SIZE = 128

N_HEADS = 8
HEAD_DIM = D_MODEL // N_HEADS
SB_WIDTH = N_HEADS * HEAD_DIM
Q_BLOCK = 128
SB_BIAS_INIT = -6.0
GM_GROUPS = 8
GM_CH = D_MODEL // GM_GROUPS
GM_WIDTH = GM_GROUPS * GM_CH
CHUNK = 128
PEER_HEADS = 8
N_KEYS = 128
N_EXPERTS = N_KEYS * N_KEYS
D_QUERY = 256
D_SUBKEY = D_QUERY // 2
TOPK = 16
PEER_BLOCK = 256
N_MOD = 6
EPS = 1e-6
OFF_Q = 0
OFF_K = OFF_Q + SB_WIDTH
OFF_V = OFF_K + SB_WIDTH
OFF_U = OFF_V + SB_WIDTH
OFF_GV = OFF_U + GM_WIDTH
OFF_GATE_A = OFF_GV + GM_WIDTH
OFF_GATE_B = OFF_GATE_A + D_MODEL
IN_WIDTH = OFF_GATE_B + D_MODEL

kernel_name = 'stick_gmlp_peer_hybrid_step'


def _rmsnorm(x, g):
    xf = x.astype(jnp.float32)
    y = xf * lax.rsqrt(jnp.mean(xf * xf, axis=-1, keepdims=True) + EPS)
    return y.astype(x.dtype) * g


def _layernorm(x, g, b):
    xf = x.astype(jnp.float32)
    mu = jnp.mean(xf, axis=-1, keepdims=True)
    var = jnp.mean(jnp.square(xf - mu), axis=-1, keepdims=True)
    return ((xf - mu) * lax.rsqrt(var + EPS)).astype(x.dtype) * g + b


def _stick_breaking_block(q, k, v, q_pos, k_pos, sb_bias):
    z = jnp.einsum('bqhd,bkhd->bhqk', q, k).astype(jnp.float32) * (HEAD_DIM ** -0.5)
    z = z + sb_bias.astype(jnp.float32)[None, :, None, None]
    readable = k_pos[None, :] < q_pos[:, None]
    log_keep = jnp.where(readable, jax.nn.log_sigmoid(-z), 0.0)
    log_after = lax.cumsum(log_keep, axis=3, reverse=True) - log_keep
    weights = jnp.where(readable, jnp.exp(jax.nn.log_sigmoid(z) + log_after), 0.0)
    return jnp.einsum('bhqk,bkhd->bqhd', weights.astype(v.dtype), v)


def _stick_breaking_attention(q, k, v, q_start, sb_bias):
    tq = q.shape[1]
    outs = []
    for s0 in range(0, tq, Q_BLOCK):
        s1 = min(s0 + Q_BLOCK, tq)
        k_end = q_start + s1
        q_pos = jnp.arange(q_start + s0, q_start + s1, dtype=jnp.int32)
        k_pos = jnp.arange(k_end, dtype=jnp.int32)
        outs.append(_stick_breaking_block(q[:, s0:s1], k[:, :k_end], v[:, :k_end], q_pos, k_pos, sb_bias))
    return jnp.concatenate(outs, axis=1)


def _chunk_spatial_gate(u, v, w_s, b_s):
    b, t, g, c = v.shape
    pad = (-t) % CHUNK
    n = (t + pad) // CHUNK
    vc = jnp.pad(v, ((0, 0), (0, pad), (0, 0), (0, 0))).reshape(b, n, CHUNK, g, c)
    w = w_s * jnp.tril(jnp.ones((CHUNK, CHUNK), w_s.dtype))
    mixed = jnp.einsum('gts,bnsgc->bntgc', w, vc) + b_s.T[:, :, None]
    return u * mixed.reshape(b, n * CHUNK, g, c)[:, :t]


def _peer_ffn(h, w_query, sub_keys, expert_down, expert_up):
    shape = h.shape
    x = h.reshape(-1, D_MODEL)
    n = x.shape[0]
    pad = (-n) % PEER_BLOCK
    blocks = jnp.pad(x, ((0, pad), (0, 0))).reshape(-1, PEER_BLOCK, D_MODEL)

    def one_block(xb):
        q = (xb @ w_query).reshape(PEER_BLOCK, PEER_HEADS, 2, D_SUBKEY)
        s = jnp.einsum('nhpd,pkd->nhpk', q, sub_keys).astype(jnp.float32)
        top_s, top_i = lax.top_k(s, TOPK)
        cand_s = (top_s[:, :, 0, :, None] + top_s[:, :, 1, None, :]).reshape(PEER_BLOCK, PEER_HEADS, TOPK * TOPK)
        cand_e = (top_i[:, :, 0, :, None] * N_KEYS + top_i[:, :, 1, None, :]).reshape(PEER_BLOCK, PEER_HEADS, TOPK * TOPK)
        best_s, best_j = lax.top_k(cand_s, TOPK)
        experts = jnp.take_along_axis(cand_e, best_j, axis=-1)
        gates = jax.nn.softmax(best_s, axis=-1)
        act = jax.nn.gelu(jnp.einsum('nd,nhkd->nhk', xb, expert_down[experts]), approximate=False)
        coef = (gates * act.astype(jnp.float32)).astype(xb.dtype)
        return jnp.einsum('nhk,nhkd->nd', coef, expert_up[experts])

    out = lax.map(one_block, blocks)
    return out.reshape(-1, D_MODEL)[:n].reshape(shape)


def _token_mixing(h, k_past, v_past, q_start, w_in, sb_bias, g_v, b_v, w_s, b_s, w_ba, w_bb, w_out):
    b, t, _ = h.shape
    proj = h @ w_in
    q = proj[..., OFF_Q:OFF_K].reshape(b, t, N_HEADS, HEAD_DIM)
    k = proj[..., OFF_K:OFF_V].reshape(b, t, N_HEADS, HEAD_DIM)
    v = proj[..., OFF_V:OFF_U].reshape(b, t, N_HEADS, HEAD_DIM)
    u = jax.nn.gelu(proj[..., OFF_U:OFF_GV], approximate=False).reshape(b, t, GM_GROUPS, GM_CH)
    gv = _layernorm(jax.nn.gelu(proj[..., OFF_GV:OFF_GATE_A], approximate=False), g_v, b_v).reshape(b, t, GM_GROUPS, GM_CH)
    gate_a = jax.nn.sigmoid(proj[..., OFF_GATE_A:OFF_GATE_B])
    gate_b = jax.nn.sigmoid(proj[..., OFF_GATE_B:IN_WIDTH])
    if k_past is None:
        keys, vals = k, v
    else:
        keys = jnp.concatenate([k_past, k], axis=1)
        vals = jnp.concatenate([v_past, v], axis=1)
    out_b = _stick_breaking_attention(q, keys, vals, q_start, sb_bias).reshape(b, t, SB_WIDTH)
    out_a = _chunk_spatial_gate(u, gv, w_s, b_s).reshape(b, t, GM_WIDTH)
    merged = gate_a * (out_a @ w_ba) + gate_b * (out_b @ w_bb)
    return merged @ w_out, k, v, gv


def _decoder_layer(x, c, k_past, v_past, q_start, params):
    (w_ada, b_ada, g1, w_in, sb_bias, g_v, b_v, w_s, b_s, w_ba, w_bb, w_out,
     g2, w_query, sub_keys, e_down, e_up) = params
    mod = (jax.nn.silu(c) @ w_ada + b_ada).reshape(c.shape[0], N_MOD, D_MODEL)
    shift1, scale1, gate1 = mod[:, 0, None, :], mod[:, 1, None, :], mod[:, 2, None, :]
    shift2, scale2, gate2 = mod[:, 3, None, :], mod[:, 4, None, :], mod[:, 5, None, :]
    h = _rmsnorm(x, g1) * (1.0 + scale1) + shift1
    mix, k_new, v_new, gv_new = _token_mixing(h, k_past, v_past, q_start, w_in, sb_bias, g_v, b_v, w_s, b_s, w_ba, w_bb, w_out)
    x = x + gate1 * mix
    h = _rmsnorm(x, g2) * (1.0 + scale2) + shift2
    x = x + gate2 * _peer_ffn(h, w_query, sub_keys, e_down, e_up)
    return x, k_new, v_new, gv_new


def setup_inputs(seed: int = 0) -> dict:
    key = jax.random.key(seed)
    ks = jax.random.split(key, 26)
    n_pages = PAST_LEN // PAGE_SIZE
    n_used = DEC_BATCH * n_pages
    n_phys = n_used + max(1, n_used // 4)

    def nrm(k, shape, scale):
        return jax.random.normal(k, shape, jnp.float32) * scale

    page_table = jax.random.permutation(ks[4], n_phys)[:n_used].reshape(DEC_BATCH, n_pages).astype(jnp.int32)
    L = DEPTH
    return {
        'x_prompt': nrm(ks[0], (BATCH, SEQ, D_MODEL), 1.0),
        'x_sample': nrm(ks[1], (DEC_BATCH, DEC_SEQ, D_MODEL), 1.0),
        'cache_k': nrm(ks[2], (L, n_phys, PAGE_SIZE, N_HEADS, HEAD_DIM), 1.0),
        'cache_v': nrm(ks[3], (L, n_phys, PAGE_SIZE, N_HEADS, HEAD_DIM), 1.0),
        'page_table': page_table,
        'c_prompt': nrm(ks[5], (BATCH, D_MODEL), 1.0),
        'c_sample': nrm(ks[6], (DEC_BATCH, D_MODEL), 1.0),
        'w_ada': nrm(ks[7], (L, D_MODEL, N_MOD * D_MODEL), 0.5 * D_MODEL ** -0.5),
        'b_ada': nrm(ks[8], (L, N_MOD * D_MODEL), 0.02),
        'g_norm1': 1.0 + nrm(ks[9], (L, D_MODEL), 0.02),
        'w_in': nrm(ks[10], (L, D_MODEL, IN_WIDTH), D_MODEL ** -0.5),
        'sb_bias': SB_BIAS_INIT + nrm(ks[24], (L, N_HEADS), 0.1),
        'g_v': 1.0 + nrm(ks[11], (L, GM_WIDTH), 0.02),
        'b_v': nrm(ks[12], (L, GM_WIDTH), 0.02),
        'w_spatial': nrm(ks[13], (L, GM_GROUPS, CHUNK, CHUNK), CHUNK ** -0.5),
        'b_spatial': 1.0 + nrm(ks[14], (L, GM_GROUPS, CHUNK), 0.02),
        'w_branch_a': nrm(ks[15], (L, GM_WIDTH, D_MODEL), GM_WIDTH ** -0.5),
        'w_branch_b': nrm(ks[16], (L, SB_WIDTH, D_MODEL), SB_WIDTH ** -0.5),
        'w_out': nrm(ks[17], (L, D_MODEL, D_MODEL), D_MODEL ** -0.5),
        'g_norm2': 1.0 + nrm(ks[18], (L, D_MODEL), 0.02),
        'w_query': nrm(ks[19], (L, D_MODEL, PEER_HEADS * D_QUERY), D_MODEL ** -0.5),
        'sub_keys': nrm(ks[20], (L, 2, N_KEYS, D_SUBKEY), D_SUBKEY ** -0.5),
        'expert_down': nrm(ks[21], (L, N_EXPERTS, D_MODEL), D_MODEL ** -0.5),
        'expert_up': nrm(ks[22], (L, N_EXPERTS, D_MODEL), 0.5),
        'g_final': 1.0 + nrm(ks[23], (D_MODEL,), 0.02),
    }


def reference(x_prompt, x_sample, cache_k, cache_v, page_table, c_prompt, c_sample,
              w_ada, b_ada, g_norm1, w_in, sb_bias, g_v, b_v, w_spatial, b_spatial,
              w_branch_a, w_branch_b, w_out, g_norm2, w_query, sub_keys,
              expert_down, expert_up, g_final):
    n_dec, n_pages = page_table.shape
    past_len = n_pages * cache_k.shape[2]
    xp, xs = x_prompt, x_sample
    kp_rows, vp_rows, ks_rows, vs_rows, gs_rows = [], [], [], [], []
    for l in range(DEPTH):
        params = (w_ada[l], b_ada[l], g_norm1[l], w_in[l], sb_bias[l], g_v[l], b_v[l], w_spatial[l], b_spatial[l],
                  w_branch_a[l], w_branch_b[l], w_out[l], g_norm2[l], w_query[l], sub_keys[l],
                  expert_down[l], expert_up[l])
        xp, kp, vp, _ = _decoder_layer(xp, c_prompt, None, None, 0, params)
        k_past = cache_k[l][page_table].reshape(n_dec, past_len, N_HEADS, HEAD_DIM)
        v_past = cache_v[l][page_table].reshape(n_dec, past_len, N_HEADS, HEAD_DIM)
        xs, ks_, vs_, gs_ = _decoder_layer(xs, c_sample, k_past, v_past, past_len, params)
        kp_rows.append(kp)
        vp_rows.append(vp)
        ks_rows.append(ks_)
        vs_rows.append(vs_)
        gs_rows.append(gs_)
    y_prompt = _rmsnorm(xp, g_final)
    y_sample = _rmsnorm(xs, g_final)
    new_k_prompt = jnp.stack(kp_rows)
    new_v_prompt = jnp.stack(vp_rows)
    new_k_sample = jnp.stack(ks_rows)
    new_v_sample = jnp.stack(vs_rows)
    new_gmlp_v_sample = jnp.stack(gs_rows)
    return (y_prompt, y_sample, new_k_prompt, new_v_prompt, new_k_sample, new_v_sample, new_gmlp_v_sample)
```

```python
import functools
import math

import numpy as np
import jax
import jax.numpy as jnp
from jax import lax
from jax.experimental import pallas as pl
from jax.experimental.pallas import tpu as pltpu

F32 = jnp.float32
BF16 = jnp.bfloat16

N_HEADS = 8
GM_GROUPS = 8
CHUNK = 128
PEER_HEADS = 8
TOPK = 16
N_MOD = 6
N_SEC = 7
EPS = 1e-6
INV_SQRT2 = 0.7071067811865476
NEG_INF = float("-inf")

ROW_TILE = 256
ATT_BLOCK = 256
Q_PAD = 16
PEER_TOK = 256
PEER_ECHUNK = 1024
TOPK_TOK = 128
MIB = 1024 * 1024


def _cparams(semantics, vmem_mib):
    return pltpu.CompilerParams(dimension_semantics=semantics, vmem_limit_bytes=vmem_mib * MIB)


def _const_spec(shape):
    nd = len(shape)
    return pl.BlockSpec(shape, lambda *_: (0,) * nd, pipeline_mode=pl.Buffered(1))


def _gelu(x):
    return 0.5 * x * (1.0 + lax.erf(x * INV_SQRT2))


def _rms_mod(x, g, scale, shift):
    y = x * lax.rsqrt(jnp.mean(x * x, axis=-1, keepdims=True) + EPS)
    return (y * g) * (1.0 + scale) + shift


def _ada_body(c_ref, w_ref, b_ref, o_ref):
    c = c_ref[...]
    s = c * jax.nn.sigmoid(c)
    o_ref[...] = jnp.dot(s.astype(BF16), w_ref[...].astype(BF16), preferred_element_type=F32) + b_ref[...]


def _adaln(c_all, w_ada, b_ada):
    nb, d = c_all.shape
    return pl.pallas_call(
        _ada_body,
        out_shape=jax.ShapeDtypeStruct((nb, N_MOD * d), F32),
        grid=(N_MOD,),
        in_specs=[
            pl.BlockSpec((nb, d), lambda j: (0, 0)),
            pl.BlockSpec((d, d), lambda j: (0, j)),
            pl.BlockSpec((1, d), lambda j: (0, j)),
        ],
        out_specs=pl.BlockSpec((nb, d), lambda j: (0, j)),
        compiler_params=_cparams(("arbitrary",), 32),
        name="adaln",
    )(c_all, w_ada, b_ada.reshape(1, -1))


def _inproj_body(x_ref, shift_ref, scale_ref, g1_ref, w_ref, gvg_ref, gvb_ref,
                 q16_ref, k_ref, k16_ref, v_ref, v16_ref, u_ref, gv_ref, ga_ref, gb_ref):
    tb, tt, d = x_ref.shape
    h = _rms_mod(x_ref[...], g1_ref[...], scale_ref[...], shift_ref[...])
    h16 = h.reshape(tb * tt, d).astype(BF16)

    def proj(sec):
        return jnp.dot(h16, w_ref[:, sec * d:(sec + 1) * d], preferred_element_type=F32)

    def put(ref, val):
        ref[...] = val.reshape(tb, tt, d).astype(ref.dtype)

    put(q16_ref, proj(0))
    k = proj(1)
    put(k_ref, k)
    put(k16_ref, k)
    v = proj(2)
    put(v_ref, v)
    put(v16_ref, v)
    put(u_ref, _gelu(proj(3)))
    g = _gelu(proj(4))
    mu = jnp.mean(g, axis=-1, keepdims=True)
    gc = g - mu
    var = jnp.mean(gc * gc, axis=-1, keepdims=True)
    put(gv_ref, (gc * lax.rsqrt(var + EPS)) * gvg_ref[...] + gvb_ref[...])
    put(ga_ref, jax.nn.sigmoid(proj(5)))
    put(gb_ref, jax.nn.sigmoid(proj(6)))


def _inproj(x, shift, scale, g1, w_in16, g_v, b_v, tb, tt):
    nb, t, d = x.shape
    grid = (nb // tb, t // tt)
    tok = pl.BlockSpec((tb, tt, d), lambda i, j: (i, j, 0))
    mod = pl.BlockSpec((tb, 1, d), lambda i, j: (i, 0, 0))
    outs = [BF16, F32, BF16, F32, BF16, F32, F32, F32, F32]
    return pl.pallas_call(
        _inproj_body,
        out_shape=[jax.ShapeDtypeStruct((nb, t, d), dt) for dt in outs],
        grid=grid,
        in_specs=[tok, mod, mod, _const_spec((1, d)), _const_spec(w_in16.shape),
                  _const_spec((1, d)), _const_spec((1, d))],
        out_specs=[tok] * len(outs),
        compiler_params=_cparams(("parallel", "parallel"), 56),
        name="inproj",
    )(x, shift, scale, g1.reshape(1, d), w_in16, g_v.reshape(1, d), b_v.reshape(1, d))


def _sb_weights(z, readable, carry, tri2):
    lk = -(jnp.maximum(z, 0.0) + jnp.log1p(jnp.exp(-jnp.abs(z))))
    if readable is not None:
        lk = jnp.where(readable, lk, 0.0)
    hi = lk.astype(BF16)
    lo = (lk - hi.astype(F32)).astype(BF16)
    la = jnp.dot(jnp.concatenate([hi, lo], axis=1), tri2, preferred_element_type=F32) + carry
    w = jnp.exp(z + lk + la)
    if readable is not None:
        w = jnp.where(readable, w, 0.0)
    return w, carry + jnp.sum(lk, axis=1, keepdims=True)


def _attn_prompt_body(q_ref, k_ref, v_ref, tri_ref, bias_ref, o_ref, *, scale):
    qi = pl.program_id(2)
    tq = q_ref.shape[1]
    q = q_ref[0]
    tri2 = tri_ref[...]
    bias = bias_ref[0]

    def logits(kb):
        k = k_ref[0, pl.ds(pl.multiple_of(kb * tq, tq), tq), :]
        z = lax.dot_general(q, k, (((1,), (1,)), ((), ())), preferred_element_type=F32)
        return z * scale + bias

    def values(kb):
        return v_ref[0, pl.ds(pl.multiple_of(kb * tq, tq), tq), :]

    row = lax.broadcasted_iota(jnp.int32, (tq, tq), 0)
    col = lax.broadcasted_iota(jnp.int32, (tq, tq), 1)
    w, carry = _sb_weights(logits(qi), col < row, jnp.zeros((tq, 1), F32), tri2)
    acc = jnp.dot(w.astype(BF16), values(qi), preferred_element_type=F32)

    def step(i, state):
        carry, acc = state
        kb = qi - 1 - i
        w, carry = _sb_weights(logits(kb), None, carry, tri2)
        return carry, acc + jnp.dot(w.astype(BF16), values(kb), preferred_element_type=F32)

    _, acc = lax.fori_loop(0, qi, step, (carry, acc))
    o_ref[0] = acc.astype(o_ref.dtype)


def _tri2(tk):
    u = (np.arange(tk)[:, None] > np.arange(tk)[None, :]).astype(np.float32)
    return jnp.asarray(np.concatenate([u, u], axis=0), dtype=BF16)


def _attn_prompt(q16, k16, v16, sb_bias):
    b, t, d = q16.shape
    hd = d // N_HEADS
    tq = ATT_BLOCK
    bias = jnp.broadcast_to(sb_bias.astype(F32)[:, None, None], (N_HEADS, 1, tq))
    qspec = pl.BlockSpec((1, tq, hd), lambda bi, h, qi: (bi, qi, h))
    kvspec = pl.BlockSpec((1, t, hd), lambda bi, h, qi: (bi, 0, h))
    return pl.pallas_call(
        functools.partial(_attn_prompt_body, scale=hd ** -0.5),
        out_shape=jax.ShapeDtypeStruct((b, t, d), BF16),
        grid=(b, N_HEADS, t // tq),
        in_specs=[qspec, kvspec, kvspec, _const_spec((2 * tq, tq)),
                  pl.BlockSpec((1, 1, tq), lambda bi, h, qi: (h, 0, 0))],
        out_specs=qspec,
        compiler_params=_cparams(("parallel", "parallel", "arbitrary"), 32),
        name="attn_prompt",
    )(q16, k16, v16, _tri2(tq), bias)


def _attn_sample_body(pt_ref, q_ref, kn_ref, vn_ref, kc_ref, vc_ref, tri_ref, bias_ref,
                      o_ref, carry_ref, acc_ref, *, scale):
    del pt_ref
    p = pl.program_id(1)
    n_new, d = kn_ref.shape[1], kn_ref.shape[2]
    page, hd = kc_ref.shape[1], kc_ref.shape[3]
    rows = N_HEADS * Q_PAD
    q16 = q_ref[0].astype(BF16)
    tri2 = tri_ref[...]

    def block(get_k, get_v, readable):
        zs = []
        for h in range(N_HEADS):
            zs.append(lax.dot_general(q16[:, h * hd:(h + 1) * hd], get_k(h), (((1,), (1,)), ((), ())),
                                      preferred_element_type=F32))
        z = jnp.concatenate(zs, axis=0) * scale + bias_ref[...]
        w, carry = _sb_weights(z, readable, carry_ref[...], tri2)
        carry_ref[...] = carry
        w16 = w.astype(BF16)
        for h in range(N_HEADS):
            acc_ref[h] += jnp.dot(w16[h * Q_PAD:(h + 1) * Q_PAD], get_v(h), preferred_element_type=F32)

    @pl.when(p == 0)
    def _():
        carry_ref[...] = jnp.zeros_like(carry_ref)
        acc_ref[...] = jnp.zeros_like(acc_ref)
        pad = jnp.zeros((page - n_new, hd), F32)

        def new_rows(ref, h):
            return jnp.concatenate([ref[0, :, h * hd:(h + 1) * hd], pad], axis=0).astype(BF16)

        qpos = lax.broadcasted_iota(jnp.int32, (rows, page), 0) % Q_PAD
        kpos = lax.broadcasted_iota(jnp.int32, (rows, page), 1)
        block(lambda h: new_rows(kn_ref, h), lambda h: new_rows(vn_ref, h), kpos < qpos)

    block(lambda h: kc_ref[0, :, h, :].astype(BF16), lambda h: vc_ref[0, :, h, :].astype(BF16), None)

    @pl.when(p == pl.num_programs(1) - 1)
    def _():
        for h in range(N_HEADS):
            o_ref[0, :, h * hd:(h + 1) * hd] = acc_ref[h][:n_new, :]


def _attn_sample(q16, k_new, v_new, cache_k, cache_v, page_table, sb_bias):
    nb, n_new, d = q16.shape
    n_pages = page_table.shape[1]
    page, hd = cache_k.shape[1], cache_k.shape[3]
    rows = N_HEADS * Q_PAD
    qpad = jnp.pad(q16.astype(F32), ((0, 0), (0, Q_PAD - n_new), (0, 0)))
    bias = jnp.broadcast_to(jnp.repeat(sb_bias.astype(F32), Q_PAD)[:, None], (rows, page))
    new_spec = pl.BlockSpec((1, n_new, d), lambda b, p, pt: (b, 0, 0))
    page_spec = pl.BlockSpec((1, page, N_HEADS, hd), lambda b, p, pt: (pt[b, n_pages - 1 - p], 0, 0, 0))
    grid_spec = pltpu.PrefetchScalarGridSpec(
        num_scalar_prefetch=1,
        grid=(nb, n_pages),
        in_specs=[pl.BlockSpec((1, Q_PAD, d), lambda b, p, pt: (b, 0, 0)), new_spec, new_spec,
                  page_spec, page_spec,
                  pl.BlockSpec((2 * page, page), lambda b, p, pt: (0, 0)),
                  pl.BlockSpec((rows, page), lambda b, p, pt: (0, 0))],
        out_specs=new_spec,
        scratch_shapes=[pltpu.VMEM((rows, 1), F32), pltpu.VMEM((N_HEADS, Q_PAD, hd), F32)],
    )
    return pl.pallas_call(
        functools.partial(_attn_sample_body, scale=hd ** -0.5),
        out_shape=jax.ShapeDtypeStruct((nb, n_new, d), F32),
        grid_spec=grid_spec,
        compiler_params=_cparams(("parallel", "arbitrary"), 32),
        name="attn_sample",
    )(page_table, qpad, k_new, v_new, cache_k, cache_v, _tri2(page), bias)


def _spatial_gate_chunks(gv, w_ref, bs_ref):
    r, d = gv.shape
    gc = d // GM_GROUPS
    row = lax.broadcasted_iota(jnp.int32, (CHUNK, CHUNK), 0)
    col = lax.broadcasted_iota(jnp.int32, (CHUNK, CHUNK), 1)
    gv16 = gv.astype(BF16)
    cols = []
    for g in range(GM_GROUPS):
        w = jnp.where(col <= row, w_ref[g], 0.0).astype(BF16)
        parts = [jnp.dot(w, gv16[c * CHUNK:(c + 1) * CHUNK, g * gc:(g + 1) * gc], preferred_element_type=F32)
                 + bs_ref[:, g:g + 1] for c in range(r // CHUNK)]
        cols.append(jnp.concatenate(parts, axis=0))
    return jnp.concatenate(cols, axis=1)


def _spatial_gate_partial(gv3, wc_ref, bs_ref):
    tb, tt, d = gv3.shape
    gc = d // GM_GROUPS
    t_idx = lax.broadcasted_iota(jnp.int32, (tt, 1), 0)
    cols = []
    for g in range(GM_GROUPS):
        gvg = gv3[:, :, g * gc:(g + 1) * gc]
        mixed = jnp.broadcast_to(bs_ref[:, g:g + 1][None], (tb, tt, gc))
        for s in range(tt):
            wcol = jnp.where(t_idx >= s, wc_ref[g, s], 0.0)
            mixed = mixed + wcol[None] * gvg[:, s:s + 1, :]
        cols.append(mixed)
    return jnp.concatenate(cols, axis=2).reshape(tb * tt, d)


def _mix_body(u_ref, gv_ref, ga_ref, gb_ref, at_ref, x_ref, gate1_ref, shift2_ref, scale2_ref, g2_ref,
              wsp_ref, bs_ref, wba_ref, wbb_ref, wout_ref, x1_ref, h2_ref):
    tb, tt, d = x_ref.shape
    r = tb * tt
    if tt % CHUNK == 0:
        mixed = _spatial_gate_chunks(gv_ref[...].reshape(r, d), wsp_ref, bs_ref)
    else:
        mixed = _spatial_gate_partial(gv_ref[...], wsp_ref, bs_ref)
    out_a = (u_ref[...].reshape(r, d) * mixed).astype(BF16)
    out_b = at_ref[...].reshape(r, d).astype(BF16)
    merged = (ga_ref[...].reshape(r, d) * jnp.dot(out_a, wba_ref[...], preferred_element_type=F32)
              + gb_ref[...].reshape(r, d) * jnp.dot(out_b, wbb_ref[...], preferred_element_type=F32))
    mix = jnp.dot(merged.astype(BF16), wout_ref[...], preferred_element_type=F32).reshape(tb, tt, d)
    x1 = x_ref[...] + gate1_ref[...] * mix
    x1_ref[...] = x1
    h2_ref[...] = _rms_mod(x1, g2_ref[...], scale2_ref[...], shift2_ref[...]).astype(h2_ref.dtype)


def _mix(u, gv, ga, gb, attn, x, gate1, shift2, scale2, g2, w_spatial, b_spatial, wba16, wbb16, wout16, tb, tt):
    nb, t, d = x.shape
    tok = pl.BlockSpec((tb, tt, d), lambda i, j: (i, j, 0))
    mod = pl.BlockSpec((tb, 1, d), lambda i, j: (i, 0, 0))
    if tt % CHUNK == 0:
        wsp = w_spatial
        bs = b_spatial.T
    else:
        wsp = jnp.swapaxes(w_spatial[:, :tt, :tt], 1, 2)[..., None]
        bs = b_spatial.T[:tt]
    return pl.pallas_call(
        _mix_body,
        out_shape=[jax.ShapeDtypeStruct((nb, t, d), F32), jax.ShapeDtypeStruct((nb, t, d), BF16)],
        grid=(nb // tb, t // tt),
        in_specs=[tok] * 6 + [mod] * 3 + [_const_spec((1, d)), _const_spec(wsp.shape), _const_spec(bs.shape),
                                          _const_spec(wba16.shape), _const_spec(wbb16.shape),
                                          _const_spec(wout16.shape)],
        out_specs=[tok, tok],
        compiler_params=_cparams(("parallel", "parallel"), 48),
        name="mix",
    )(u, gv, ga, gb, attn, x, gate1, shift2, scale2, g2.reshape(1, d), wsp, bs, wba16, wbb16, wout16)


def _scores_body(h2_ref, wq_ref, sk_ref, st_ref, h2t_ref):
    h2 = h2_ref[...]
    nsk = sk_ref.shape[2]
    q16 = jnp.dot(h2, wq_ref[...], preferred_element_type=F32).astype(BF16)
    for hp in range(st_ref.shape[0]):
        st_ref[hp] = lax.dot_general(sk_ref[hp % 2], q16[:, hp * nsk:(hp + 1) * nsk],
                                     (((1,), (1,)), ((), ())), preferred_element_type=F32)
    h2t_ref[...] = h2.astype(F32).T.astype(h2t_ref.dtype)


def _scores(h2, wq16, sk16):
    n, d = h2.shape
    n_keys, dsk = sk16.shape[1], sk16.shape[2]
    nhp = wq16.shape[1] // dsk
    r = ROW_TILE
    return pl.pallas_call(
        _scores_body,
        out_shape=[jax.ShapeDtypeStruct((nhp, n_keys, n), F32), jax.ShapeDtypeStruct((d, n), BF16)],
        grid=(n // r,),
        in_specs=[pl.BlockSpec((r, d), lambda i: (i, 0)), _const_spec(wq16.shape), _const_spec(sk16.shape)],
        out_specs=[pl.BlockSpec((nhp, n_keys, r), lambda i: (0, 0, i)), pl.BlockSpec((d, r), lambda i: (0, i))],
        compiler_params=_cparams(("parallel",), 32),
        name="peer_scores",
    )(h2, wq16, sk16)


def _top16(x, iota):
    n = float(x.shape[0])
    rank = jnp.full(x.shape, float(TOPK), F32)
    vals = []
    for r in range(TOPK):
        m = jnp.max(x, axis=0, keepdims=True)
        first = jnp.min(jnp.where(x == m, iota, n), axis=0, keepdims=True)
        sel = iota == first
        rank = jnp.where(sel, float(r), rank)
        x = jnp.where(sel, NEG_INF, x)
        vals.append(m)
    return jnp.concatenate(vals, axis=0), rank


def _cand_layout():
    groups = [(0, list(range(16)))] + [(a, list(range(8))) for a in range(1, 8)]
    fidx = []
    for a, bs in groups:
        fidx += [a * TOPK + b if (a + 1) * (b + 1) <= TOPK else -1 for b in bs]
    fidx += [a * TOPK for a in range(8, 16)]
    return np.asarray(fidx, np.float32)


def _topk_body(st_ref, fidx_ref, a_ref, c0_ref, b_ref, r1_ref):
    n_keys, tok = st_ref.shape[1], st_ref.shape[2]
    iota = lax.broadcasted_iota(jnp.int32, (n_keys, tok), 0).astype(F32)
    fidx = fidx_ref[...]
    big = float(TOPK * TOPK)

    def head(h, _):
        s0 = st_ref[2 * h]
        s1 = st_ref[2 * h + 1]
        t0, rank0 = _top16(s0, iota)
        t1, rank1 = _top16(s1, iota)
        rows = [t0[0:1] + t1]
        rows += [t0[a:a + 1] + t1[0:8] for a in range(1, 8)]
        rows += [t0[8:16] + t1[0:1]]
        cand = jnp.where(fidx >= 0, jnp.concatenate(rows, axis=0), NEG_INF)
        taken = jnp.zeros(cand.shape, F32)
        zsum = jnp.zeros((1, tok), F32)
        top = t0[0:1] + t1[0:1]
        for _r in range(TOPK):
            m = jnp.max(cand, axis=0, keepdims=True)
            first = jnp.min(jnp.where((cand == m) & (fidx >= 0), fidx, big), axis=0, keepdims=True)
            sel = fidx == first
            taken = jnp.where(sel, 1.0, taken)
            cand = jnp.where(sel, NEG_INF, cand)
            zsum = zsum + jnp.exp(m - top)
        counts = [jnp.sum(taken[0:16], axis=0, keepdims=True)]
        counts += [jnp.sum(taken[8 + 8 * a:16 + 8 * a], axis=0, keepdims=True) for a in range(1, 8)]
        counts.append(taken[72:80])
        cnt = jnp.concatenate(counts, axis=0)
        c0 = jnp.zeros((n_keys, tok), F32)
        for a in range(TOPK):
            c0 = jnp.where(rank0 == float(a), cnt[a:a + 1], c0)
        a_ref[h] = jnp.exp(s0 - t0[0:1]) / zsum
        c0_ref[h] = c0
        b_ref[h] = jnp.exp(s1 - t1[0:1])
        r1_ref[h] = rank1
        return 0

    lax.fori_loop(0, a_ref.shape[0], head, 0)


def _topk(st):
    nhp, n_keys, n = st.shape
    nh = nhp // 2
    tok = TOPK_TOK
    fidx = jnp.asarray(np.broadcast_to(_cand_layout()[:, None], (80, tok)))
    out = jax.ShapeDtypeStruct((nh, n_keys, n), F32)
    ospec = pl.BlockSpec((nh, n_keys, tok), lambda i: (0, 0, i))
    return pl.pallas_call(
        _topk_body,
        out_shape=[out] * 4,
        grid=(n // tok,),
        in_specs=[pl.BlockSpec((nhp, n_keys, tok), lambda i: (0, 0, i)), _const_spec((80, tok))],
        out_specs=[ospec] * 4,
        compiler_params=_cparams(("parallel",), 32),
        name="peer_topk",
    )(st, fidx)


def _peer_body(h2t_ref, a_ref, c0_ref, b_ref, r1_ref, down_ref, upt_ref, x1_ref, gate2_ref, gf_ref,
               y_ref, acc_ref, coef_ref, *, final_norm):
    c = pl.program_id(1)
    nh, ic, tok = a_ref.shape
    n_keys = b_ref.shape[1]

    @pl.when(c == 0)
    def _():
        acc_ref[...] = jnp.zeros_like(acc_ref)

    act = jnp.dot(down_ref[...], h2t_ref[...], preferred_element_type=F32)
    for il in range(ic):
        gate = jnp.zeros((n_keys, tok), F32)
        for h in range(nh):
            sel = r1_ref[h] < c0_ref[h, il:il + 1, :]
            gate = gate + jnp.where(sel, b_ref[h], 0.0) * a_ref[h, il:il + 1, :]
        coef_ref[il * n_keys:(il + 1) * n_keys, :] = (
            gate * _gelu(act[il * n_keys:(il + 1) * n_keys, :])).astype(coef_ref.dtype)
    acc_ref[...] += jnp.dot(upt_ref[...], coef_ref[...], preferred_element_type=F32)

    @pl.when(c == pl.num_programs(1) - 1)
    def _():
        tb, tt, d = x1_ref.shape
        y = x1_ref[...] + gate2_ref[...] * acc_ref[...].T.reshape(tb, tt, d)
        if final_norm:
            y = (y * lax.rsqrt(jnp.mean(y * y, axis=-1, keepdims=True) + EPS)) * gf_ref[...]
        y_ref[...] = y


def _peer(h2t, a, c0, b, r1, down16, upt16, x1, gate2, g_final, tb, tt, final_norm):
    nb, t, d = x1.shape
    nh, n_keys, n = b.shape
    tok = tb * tt
    ne = down16.shape[0]
    ech = PEER_ECHUNK
    ic = ech // n_keys
    per_tok = t // tt
    lane = lambda i, c: (0, 0, i)
    tokspec = pl.BlockSpec((tb, tt, d), lambda i, c: (i // per_tok, i % per_tok, 0))
    return pl.pallas_call(
        functools.partial(_peer_body, final_norm=final_norm),
        out_shape=jax.ShapeDtypeStruct((nb, t, d), F32),
        grid=(n // tok, ne // ech),
        in_specs=[pl.BlockSpec((d, tok), lambda i, c: (0, i)),
                  pl.BlockSpec((nh, ic, tok), lambda i, c: (0, c, i)),
                  pl.BlockSpec((nh, ic, tok), lambda i, c: (0, c, i)),
                  pl.BlockSpec((nh, n_keys, tok), lane),
                  pl.BlockSpec((nh, n_keys, tok), lane),
                  pl.BlockSpec((ech, d), lambda i, c: (c, 0)),
                  pl.BlockSpec((d, ech), lambda i, c: (0, c)),
                  tokspec,
                  pl.BlockSpec((tb, 1, d), lambda i, c: (i // per_tok, 0, 0)),
                  pl.BlockSpec((1, d), lambda i, c: (0, 0))],
        out_specs=tokspec,
        scratch_shapes=[pltpu.VMEM((d, tok), F32), pltpu.VMEM((ech, tok), BF16)],
        compiler_params=_cparams(("parallel", "arbitrary"), 48),
        name="peer",
    )(h2t, a, c0, b, r1, down16, upt16, x1, gate2, g_final.reshape(1, d))


def _transpose_body(x_ref, o_ref):
    o_ref[...] = x_ref[...].T.astype(o_ref.dtype)


def _transpose_bf16(x):
    r, c = x.shape
    blk = 1024
    return pl.pallas_call(
        _transpose_body,
        out_shape=jax.ShapeDtypeStruct((c, r), BF16),
        grid=(r // blk, c // blk),
        in_specs=[pl.BlockSpec((blk, blk), lambda i, j: (i, j))],
        out_specs=pl.BlockSpec((blk, blk), lambda i, j: (j, i)),
        compiler_params=_cparams(("parallel", "parallel"), 32),
        name="transpose_cast",
    )(x)


def _row_tiles(nb, t):
    if t % ROW_TILE == 0:
        return 1, ROW_TILE
    assert ROW_TILE % t == 0 and t % 8 == 0 and nb % (ROW_TILE // t) == 0, (nb, t)
    return ROW_TILE // t, t


def _layer(x, mod, weights, attend, final_norm, g_final):
    (g1, w_in16, g_v, b_v, w_spatial, b_spatial, wba16, wbb16, wout16, g2, wq16, sk16, down16, upt16) = weights
    nb, t, d = x.shape
    tb, tt = _row_tiles(nb, t)
    shift1, scale1, gate1, shift2, scale2, gate2 = (mod[:, i:i + 1, :] for i in range(N_MOD))
    q16, k, k16, v, v16, u, gv, ga, gb = _inproj(x, shift1, scale1, g1, w_in16, g_v, b_v, tb, tt)
    attn = attend(q16, k, k16, v, v16)
    x1, h2 = _mix(u, gv, ga, gb, attn, x, gate1, shift2, scale2, g2, w_spatial, b_spatial,
                  wba16, wbb16, wout16, tb, tt)
    st, h2t = _scores(h2.reshape(nb * t, d), wq16, sk16)
    a, c0, b, r1 = _topk(st)
    ptb, ptt = (1, PEER_TOK) if t % PEER_TOK == 0 else (PEER_TOK // t, t)
    y = _peer(h2t, a, c0, b, r1, down16, upt16, x1, gate2, g_final, ptb, ptt, final_norm)
    return y, k, v, gv


def kernel(x_prompt, x_sample, cache_k, cache_v, page_table, c_prompt, c_sample, w_ada, b_ada, g_norm1, w_in,
           sb_bias, g_v, b_v, w_spatial, b_spatial, w_branch_a, w_branch_b, w_out, g_norm2, w_query, sub_keys,
           expert_down, expert_up, g_final):
    depth = w_ada.shape[0]
    n_prompt, n_sample = c_prompt.shape[0], c_sample.shape[0]
    d = x_prompt.shape[-1]
    hd = d // N_HEADS
    c_all = jnp.concatenate([c_prompt, c_sample], axis=0)
    n_c = c_all.shape[0]
    c_all = jnp.pad(c_all, ((0, (-n_c) % 16), (0, 0)))
    xp, xs = x_prompt, x_sample
    rows = [[] for _ in range(5)]
    for l in range(depth):
        mod = _adaln(c_all, w_ada[l], b_ada[l]).reshape(-1, N_MOD, d)
        weights = (g_norm1[l], w_in[l].astype(BF16), g_v[l], b_v[l], w_spatial[l], b_spatial[l],
                   w_branch_a[l].astype(BF16), w_branch_b[l].astype(BF16), w_out[l].astype(BF16), g_norm2[l],
                   w_query[l].astype(BF16), sub_keys[l].astype(BF16), expert_down[l].astype(BF16),
                   _transpose_bf16(expert_up[l]))
        last = l == depth - 1
        xp, kp, vp, _ = _layer(
            xp, mod[:n_prompt], weights,
            lambda q16, k, k16, v, v16: _attn_prompt(q16, k16, v16, sb_bias[l]), last, g_final)
        xs, ks, vs, gs = _layer(
            xs, mod[n_prompt:n_c], weights,
            lambda q16, k, k16, v, v16: _attn_sample(q16, k, v, cache_k[l], cache_v[l], page_table, sb_bias[l]),
            last, g_final)
        for lst, val in zip(rows, (kp, vp, ks, vs, gs)):
            lst.append(val.reshape(val.shape[0], val.shape[1], N_HEADS, hd))
    if depth == 0:
        raise ValueError("depth must be positive")
    return (xp, xs) + tuple(jnp.stack(r) for r in rows)
```

```python
import functools
import math

import numpy as np
import jax
import jax.numpy as jnp
from jax import lax
from jax.experimental import pallas as pl
from jax.experimental.pallas import tpu as pltpu

F32 = jnp.float32
BF16 = jnp.bfloat16

N_HEADS = 8
GM_GROUPS = 8
CHUNK = 128
PEER_HEADS = 8
TOPK = 16
N_MOD = 6
N_SEC = 7
EPS = 1e-6
INV_SQRT2 = 0.7071067811865476
NEG_INF = float("-inf")

ROW_TILE = 256
ATT_BLOCK = 256
ATT_HEADS = 2
ATT_PAGES = 4
PEER_TOK = 512
LANES = 128
PEER_ECHUNK = 1024
PEER_PIECE = 256
BF16_SUBLANES = 16
TOPK_TOK = 128
MIB = 1024 * 1024


def _cparams(semantics, vmem_mib):
    return pltpu.CompilerParams(dimension_semantics=semantics, vmem_limit_bytes=vmem_mib * MIB)


def _const_spec(shape):
    nd = len(shape)
    return pl.BlockSpec(shape, lambda *_: (0,) * nd, pipeline_mode=pl.Buffered(1))


def _gelu(x):
    return 0.5 * x * (1.0 + lax.erf(x * INV_SQRT2))


def _rms_mod(x, g, scale, shift):
    y = x * lax.rsqrt(jnp.mean(x * x, axis=-1, keepdims=True) + EPS)
    return (y * g) * (1.0 + scale) + shift


def _ada_body(c_ref, w_ref, b_ref, o_ref):
    c = c_ref[...]
    s = c * jax.nn.sigmoid(c)
    o_ref[...] = jnp.dot(s.astype(BF16), w_ref[...].astype(BF16), preferred_element_type=F32) + b_ref[...]


def _adaln(c_all, w_ada, b_ada):
    nb, d = c_all.shape
    return pl.pallas_call(
        _ada_body,
        out_shape=jax.ShapeDtypeStruct((nb, N_MOD * d), F32),
        grid=(N_MOD,),
        in_specs=[
            pl.BlockSpec((nb, d), lambda j: (0, 0)),
            pl.BlockSpec((d, d), lambda j: (0, j)),
            pl.BlockSpec((1, d), lambda j: (0, j)),
        ],
        out_specs=pl.BlockSpec((nb, d), lambda j: (0, j)),
        compiler_params=_cparams(("arbitrary",), 32),
        name="adaln",
    )(c_all, w_ada, b_ada.reshape(1, -1))


def _inproj_body(x_ref, shift_ref, scale_ref, g1_ref, w_ref, gvg_ref, gvb_ref,
                 q16_ref, k_ref, k16_ref, v_ref, v16_ref, u_ref, gv_ref, ga_ref, gb_ref):
    tb, tt, d = x_ref.shape
    h = _rms_mod(x_ref[...], g1_ref[...], scale_ref[...], shift_ref[...])
    h16 = h.reshape(tb * tt, d).astype(BF16)

    def proj(sec):
        return jnp.dot(h16, w_ref[:, sec * d:(sec + 1) * d], preferred_element_type=F32)

    def put(ref, val):
        ref[...] = val.reshape(tb, tt, d).astype(ref.dtype)

    put(q16_ref, proj(0))
    k = proj(1)
    put(k_ref, k)
    put(k16_ref, k)
    v = proj(2)
    put(v_ref, v)
    put(v16_ref, v)
    put(u_ref, _gelu(proj(3)))
    g = _gelu(proj(4))
    mu = jnp.mean(g, axis=-1, keepdims=True)
    gc = g - mu
    var = jnp.mean(gc * gc, axis=-1, keepdims=True)
    put(gv_ref, (gc * lax.rsqrt(var + EPS)) * gvg_ref[...] + gvb_ref[...])
    put(ga_ref, jax.nn.sigmoid(proj(5)))
    put(gb_ref, jax.nn.sigmoid(proj(6)))


def _inproj(x, shift, scale, g1, w_in16, g_v, b_v, tb, tt):
    nb, t, d = x.shape
    grid = (nb // tb, t // tt)
    tok = pl.BlockSpec((tb, tt, d), lambda i, j: (i, j, 0))
    mod = pl.BlockSpec((tb, 1, d), lambda i, j: (i, 0, 0))
    outs = [BF16, F32, BF16, F32, BF16, F32, F32, F32, F32]
    return pl.pallas_call(
        _inproj_body,
        out_shape=[jax.ShapeDtypeStruct((nb, t, d), dt) for dt in outs],
        grid=grid,
        in_specs=[tok, mod, mod, _const_spec((1, d)), _const_spec(w_in16.shape),
                  _const_spec((1, d)), _const_spec((1, d))],
        out_specs=[tok] * len(outs),
        compiler_params=_cparams(("parallel", "parallel"), 56),
        name="inproj",
    )(x, shift, scale, g1.reshape(1, d), w_in16, g_v.reshape(1, d), b_v.reshape(1, d))


def _sb_weights(z, readable, carry, tri2):
    lk, la, total = _sb_log_keep(z, readable, tri2)
    return _sb_finish(z, lk, la + carry, readable), carry + total


def _sb_log_keep(z, readable, tri2):
    lk = -(jnp.maximum(z, 0.0) + jnp.log(1.0 + jnp.exp(-jnp.abs(z))))
    if readable is not None:
        lk = jnp.where(readable, lk, 0.0)
    hi = lk.astype(BF16)
    lo = (lk - hi.astype(F32)).astype(BF16)
    la = jnp.dot(jnp.concatenate([hi, lo], axis=1), tri2, preferred_element_type=F32)
    return lk, la, jnp.sum(lk, axis=1, keepdims=True)


def _sb_finish(z, lk, log_after, readable):
    w = jnp.exp(z + lk + log_after)
    if readable is not None:
        w = jnp.where(readable, w, 0.0)
    return w


def _attn_prompt_body(q_ref, k_ref, v_ref, tri_ref, bias_ref, o_ref, acc_ref, z_ref, w_ref, *, scale, hd):
    qi = pl.program_id(2)
    tq = q_ref.shape[1]
    nh = q_ref.shape[2] // hd
    tri2 = tri_ref[...]
    lanes = [slice(hh * hd, (hh + 1) * hd) for hh in range(nh)]

    def key_rows(kb):
        return pl.ds(pl.multiple_of(kb * tq, tq), tq)

    def put_logits(kb):
        rows = key_rows(kb)
        for hh in range(nh):
            z_ref[hh] = lax.dot_general(q_ref[0, :, lanes[hh]], k_ref[0, rows, lanes[hh]],
                                        (((1,), (1,)), ((), ())), preferred_element_type=F32)

    def add_values(kb):
        rows = key_rows(kb)
        for hh in range(nh):
            acc_ref[hh] += jnp.dot(w_ref[hh], v_ref[0, rows, lanes[hh]], preferred_element_type=F32)

    def put_weights(zs, readable, carries):
        out = []
        for hh in range(nh):
            w, carry = _sb_weights(zs[hh] * scale + bias_ref[0, hh], readable, carries[hh], tri2)
            w_ref[hh] = w.astype(w_ref.dtype)
            out.append(carry)
        return tuple(out)

    acc_ref[...] = jnp.zeros_like(acc_ref)
    row = lax.broadcasted_iota(jnp.int32, (tq, tq), 0)
    col = lax.broadcasted_iota(jnp.int32, (tq, tq), 1)
    put_logits(qi)
    zs = [z_ref[hh] for hh in range(nh)]
    put_logits(jnp.maximum(qi - 1, 0))
    carries = put_weights(zs, col < row, [jnp.zeros((tq, 1), F32)] * nh)

    def step(i, carries):
        kb = qi - 1 - i
        zs = [z_ref[hh] for hh in range(nh)]
        add_values(kb + 1)
        put_logits(jnp.maximum(kb - 1, 0))
        return put_weights(zs, None, carries)

    lax.fori_loop(0, qi, step, carries)
    add_values(0)
    for hh in range(nh):
        o_ref[0, :, lanes[hh]] = acc_ref[hh].astype(o_ref.dtype)


def _tri2(tk):
    u = (np.arange(tk)[:, None] > np.arange(tk)[None, :]).astype(np.float32)
    return jnp.asarray(np.concatenate([u, u], axis=0), dtype=BF16)


def _attn_prompt(q16, k16, v16, sb_bias):
    b, t, d = q16.shape
    hd = d // N_HEADS
    tq = ATT_BLOCK
    nh = ATT_HEADS
    bias = jnp.broadcast_to(sb_bias.astype(F32).reshape(N_HEADS // nh, nh, 1, 1), (N_HEADS // nh, nh, 1, tq))
    qspec = pl.BlockSpec((1, tq, nh * hd), lambda bi, h, qi: (bi, qi, h))
    kvspec = pl.BlockSpec((1, t, nh * hd), lambda bi, h, qi: (bi, 0, h))
    return pl.pallas_call(
        functools.partial(_attn_prompt_body, scale=hd ** -0.5, hd=hd),
        out_shape=jax.ShapeDtypeStruct((b, t, d), BF16),
        grid=(b, N_HEADS // nh, t // tq),
        in_specs=[qspec, kvspec, kvspec, _const_spec((2 * tq, tq)),
                  pl.BlockSpec((1, nh, 1, tq), lambda bi, h, qi: (h, 0, 0, 0))],
        out_specs=qspec,
        scratch_shapes=[pltpu.VMEM((nh, tq, hd), F32), pltpu.VMEM((nh, tq, tq), F32),
                        pltpu.VMEM((nh, tq, tq), BF16)],
        compiler_params=_cparams(("parallel", "parallel", "arbitrary"), 32),
        name="attn_prompt",
    )(q16, k16, v16, _tri2(tq), bias)


def _attn_sample_body(pt_ref, q_ref, kn_ref, vn_ref, *rest, scale, n_pp):
    del pt_ref
    kc_refs, vc_refs = rest[:n_pp], rest[n_pp:2 * n_pp]
    tri_ref, bias_ref, o_ref, carry_ref, acc_ref = rest[2 * n_pp:]
    p = pl.program_id(1)
    q16 = q_ref[0]
    rows, hd = q16.shape
    n_q = rows // N_HEADS
    lt = tri_ref.shape[1]
    tri2 = tri_ref[...]
    bias = bias_ref[...]
    row = lax.broadcasted_iota(jnp.int32, (rows, lt), 0)
    lane = lax.broadcasted_iota(jnp.int32, (rows, lt), 1)
    own_head = (lane % N_HEADS) == (row // n_q)

    def blocks(kvs, readable):
        zs = [lax.dot_general(q16, k2, (((1,), (1,)), ((), ())), preferred_element_type=F32) for k2, _ in kvs]
        carry = carry_ref[...]
        tiles = []
        for z in zs:
            row_tiles = [None] * (z.shape[1] // lt)
            for t in reversed(range(len(row_tiles))):
                zt = z[:, t * lt:(t + 1) * lt] * scale + bias
                lk, la, total = _sb_log_keep(zt, readable, tri2)
                row_tiles[t] = (zt, lk, la + carry)
                carry = carry + total
            tiles.append(row_tiles)
        carry_ref[...] = carry
        acc = acc_ref[...]
        for row_tiles, (_, v2) in zip(tiles, kvs):
            w = jnp.concatenate([_sb_finish(zt, lk, la, readable).astype(BF16) for zt, lk, la in row_tiles], axis=1)
            acc = acc + jnp.dot(w, v2, preferred_element_type=F32)
        acc_ref[...] = acc

    @pl.when(p == 0)
    def _():
        carry_ref[...] = jnp.zeros_like(carry_ref)
        acc_ref[...] = jnp.zeros_like(acc_ref)
        blocks([(kn_ref[0], vn_ref[0])], own_head & ((lane // N_HEADS) < (row % n_q)))

    def page_rows(ref):
        return ref[0].reshape(ref.shape[1] * ref.shape[2], hd).astype(BF16)

    blocks([(page_rows(kc_refs[j]), page_rows(vc_refs[j])) for j in range(n_pp)], own_head)

    @pl.when(p == pl.num_programs(1) - 1)
    def _():
        o_ref[0] = acc_ref[...]


def _attn_sample(q16, k_new, v_new, cache_k, cache_v, page_table, sb_bias):
    nb, n_new, d = q16.shape
    n_pages = page_table.shape[1]
    page, hd = cache_k.shape[1], cache_k.shape[3]
    n_pp = ATT_PAGES
    lt = 128
    rows = N_HEADS * n_new
    assert n_pages % n_pp == 0 and (N_HEADS * n_new) <= lt and lt % N_HEADS == 0

    def head_major(x, dtype):
        return x.reshape(nb, n_new, N_HEADS, hd).transpose(0, 2, 1, 3).reshape(nb, rows, hd).astype(dtype)

    def lane_rows(x):
        return jnp.pad(x.reshape(nb, n_new * N_HEADS, hd), ((0, 0), (0, lt - n_new * N_HEADS), (0, 0))).astype(BF16)

    bias = jnp.broadcast_to(jnp.repeat(sb_bias.astype(F32), n_new)[:, None], (rows, lt))
    row_spec = pl.BlockSpec((1, rows, hd), lambda b, p, pt: (b, 0, 0))
    new_spec = pl.BlockSpec((1, lt, hd), lambda b, p, pt: (b, 0, 0))

    def page_spec(j):
        return pl.BlockSpec((1, page, N_HEADS, hd),
                            lambda b, p, pt: (pt[b, n_pages - 1 - (p * n_pp + j)], 0, 0, 0))

    grid_spec = pltpu.PrefetchScalarGridSpec(
        num_scalar_prefetch=1,
        grid=(nb, n_pages // n_pp),
        in_specs=[row_spec, new_spec, new_spec] + [page_spec(j) for j in range(n_pp)] * 2
        + [pl.BlockSpec((2 * lt, lt), lambda b, p, pt: (0, 0)), pl.BlockSpec((rows, lt), lambda b, p, pt: (0, 0))],
        out_specs=row_spec,
        scratch_shapes=[pltpu.VMEM((rows, 1), F32), pltpu.VMEM((rows, hd), F32)],
    )
    out = pl.pallas_call(
        functools.partial(_attn_sample_body, scale=hd ** -0.5, n_pp=n_pp),
        out_shape=jax.ShapeDtypeStruct((nb, rows, hd), F32),
        grid_spec=grid_spec,
        compiler_params=_cparams(("parallel", "arbitrary"), 40),
        name="attn_sample",
    )(page_table, head_major(q16, BF16), lane_rows(k_new), lane_rows(v_new),
      *([cache_k] * n_pp), *([cache_v] * n_pp), _tri2(lt), bias)
    return out.reshape(nb, N_HEADS, n_new, hd).transpose(0, 2, 1, 3).reshape(nb, n_new, d)


def _spatial_gate_chunks(gv, w_ref, bs_ref):
    r, d = gv.shape
    gc = d // GM_GROUPS
    row = lax.broadcasted_iota(jnp.int32, (CHUNK, CHUNK), 0)
    col = lax.broadcasted_iota(jnp.int32, (CHUNK, CHUNK), 1)
    gv16 = gv.astype(BF16)
    cols = []
    for g in range(GM_GROUPS):
        w = jnp.where(col <= row, w_ref[g], 0.0).astype(BF16)
        parts = [jnp.dot(w, gv16[c * CHUNK:(c + 1) * CHUNK, g * gc:(g + 1) * gc], preferred_element_type=F32)
                 + bs_ref[:, g:g + 1] for c in range(r // CHUNK)]
        cols.append(jnp.concatenate(parts, axis=0))
    return jnp.concatenate(cols, axis=1)


def _spatial_gate_partial(gv3, wc_ref, bs_ref):
    tb, tt, d = gv3.shape
    gc = d // GM_GROUPS
    t_idx = lax.broadcasted_iota(jnp.int32, (tt, 1), 0)
    cols = []
    for g in range(GM_GROUPS):
        gvg = gv3[:, :, g * gc:(g + 1) * gc]
        mixed = jnp.broadcast_to(bs_ref[:, g:g + 1][None], (tb, tt, gc))
        for s in range(tt):
            wcol = jnp.where(t_idx >= s, wc_ref[g, s], 0.0)
            mixed = mixed + wcol[None] * gvg[:, s:s + 1, :]
        cols.append(mixed)
    return jnp.concatenate(cols, axis=2).reshape(tb * tt, d)


def _mix_body(u_ref, gv_ref, ga_ref, gb_ref, at_ref, x_ref, gate1_ref, shift2_ref, scale2_ref, g2_ref,
              wsp_ref, bs_ref, wba_ref, wbb_ref, wout_ref, x1_ref, h2_ref):
    tb, tt, d = x_ref.shape
    r = tb * tt
    if tt % CHUNK == 0:
        mixed = _spatial_gate_chunks(gv_ref[...].reshape(r, d), wsp_ref, bs_ref)
    else:
        mixed = _spatial_gate_partial(gv_ref[...], wsp_ref, bs_ref)
    out_a = (u_ref[...].reshape(r, d) * mixed).astype(BF16)
    out_b = at_ref[...].reshape(r, d).astype(BF16)
    merged = (ga_ref[...].reshape(r, d) * jnp.dot(out_a, wba_ref[...], preferred_element_type=F32)
              + gb_ref[...].reshape(r, d) * jnp.dot(out_b, wbb_ref[...], preferred_element_type=F32))
    mix = jnp.dot(merged.astype(BF16), wout_ref[...], preferred_element_type=F32).reshape(tb, tt, d)
    x1 = x_ref[...] + gate1_ref[...] * mix
    x1_ref[...] = x1
    h2_ref[...] = _rms_mod(x1, g2_ref[...], scale2_ref[...], shift2_ref[...]).astype(h2_ref.dtype)


def _mix(u, gv, ga, gb, attn, x, gate1, shift2, scale2, g2, w_spatial, b_spatial, wba16, wbb16, wout16, tb, tt):
    nb, t, d = x.shape
    tok = pl.BlockSpec((tb, tt, d), lambda i, j: (i, j, 0))
    mod = pl.BlockSpec((tb, 1, d), lambda i, j: (i, 0, 0))
    if tt % CHUNK == 0:
        wsp = w_spatial
        bs = b_spatial.T
    else:
        wsp = jnp.swapaxes(w_spatial[:, :tt, :tt], 1, 2)[..., None]
        bs = b_spatial.T[:tt]
    return pl.pallas_call(
        _mix_body,
        out_shape=[jax.ShapeDtypeStruct((nb, t, d), F32), jax.ShapeDtypeStruct((nb, t, d), BF16)],
        grid=(nb // tb, t // tt),
        in_specs=[tok] * 6 + [mod] * 3 + [_const_spec((1, d)), _const_spec(wsp.shape), _const_spec(bs.shape),
                                          _const_spec(wba16.shape), _const_spec(wbb16.shape),
                                          _const_spec(wout16.shape)],
        out_specs=[tok, tok],
        compiler_params=_cparams(("parallel", "parallel"), 48),
        name="mix",
    )(u, gv, ga, gb, attn, x, gate1, shift2, scale2, g2.reshape(1, d), wsp, bs, wba16, wbb16, wout16)


def _scores_body(h2_ref, wq_ref, sk_ref, st_ref, h2t_ref):
    h2 = h2_ref[...]
    nsk = sk_ref.shape[2]
    q16 = jnp.dot(h2, wq_ref[...], preferred_element_type=F32).astype(BF16)
    for hp in range(st_ref.shape[0]):
        st_ref[hp] = lax.dot_general(sk_ref[hp % 2], q16[:, hp * nsk:(hp + 1) * nsk],
                                     (((1,), (1,)), ((), ())), preferred_element_type=F32)
    h2t_ref[...] = h2.astype(F32).T.astype(h2t_ref.dtype)


def _scores(h2, wq16, sk16):
    n, d = h2.shape
    n_keys, dsk = sk16.shape[1], sk16.shape[2]
    nhp = wq16.shape[1] // dsk
    r = ROW_TILE
    return pl.pallas_call(
        _scores_body,
        out_shape=[jax.ShapeDtypeStruct((nhp, n_keys, n), F32), jax.ShapeDtypeStruct((d, n), BF16)],
        grid=(n // r,),
        in_specs=[pl.BlockSpec((r, d), lambda i: (i, 0)), _const_spec(wq16.shape), _const_spec(sk16.shape)],
        out_specs=[pl.BlockSpec((nhp, n_keys, r), lambda i: (0, 0, i)), pl.BlockSpec((d, r), lambda i: (0, i))],
        compiler_params=_cparams(("parallel",), 32),
        name="peer_scores",
    )(h2, wq16, sk16)


def _top16(x, iota):
    n = float(x.shape[0])
    rank = jnp.full(x.shape, float(TOPK), F32)
    vals = []
    for r in range(TOPK):
        m = jnp.max(x, axis=0, keepdims=True)
        first = jnp.min(jnp.where(x == m, iota, n), axis=0, keepdims=True)
        sel = iota == first
        rank = jnp.where(sel, float(r), rank)
        x = jnp.where(sel, NEG_INF, x)
        vals.append(m)
    return jnp.concatenate(vals, axis=0), rank


def _cand_layout():
    groups = [(0, list(range(16)))] + [(a, list(range(8))) for a in range(1, 8)]
    fidx = []
    for a, bs in groups:
        fidx += [a * TOPK + b if (a + 1) * (b + 1) <= TOPK else -1 for b in bs]
    fidx += [a * TOPK for a in range(8, 16)]
    return np.asarray(fidx, np.float32)


def _topk_body(st_ref, fidx_ref, a_ref, c0_ref, b_ref, r1_ref):
    n_keys, tok = st_ref.shape[1], st_ref.shape[2]
    iota = lax.broadcasted_iota(jnp.int32, (n_keys, tok), 0).astype(F32)
    fidx = fidx_ref[...]
    big = float(TOPK * TOPK)

    def head(h, _):
        s0 = st_ref[2 * h]
        s1 = st_ref[2 * h + 1]
        t0, rank0 = _top16(s0, iota)
        t1, rank1 = _top16(s1, iota)
        rows = [t0[0:1] + t1]
        rows += [t0[a:a + 1] + t1[0:8] for a in range(1, 8)]
        rows += [t0[8:16] + t1[0:1]]
        cand = jnp.where(fidx >= 0, jnp.concatenate(rows, axis=0), NEG_INF)
        taken = jnp.zeros(cand.shape, F32)
        zsum = jnp.zeros((1, tok), F32)
        top = t0[0:1] + t1[0:1]
        for _r in range(TOPK):
            m = jnp.max(cand, axis=0, keepdims=True)
            first = jnp.min(jnp.where((cand == m) & (fidx >= 0), fidx, big), axis=0, keepdims=True)
            sel = fidx == first
            taken = jnp.where(sel, 1.0, taken)
            cand = jnp.where(sel, NEG_INF, cand)
            zsum = zsum + jnp.exp(m - top)
        counts = [jnp.sum(taken[0:16], axis=0, keepdims=True)]
        counts += [jnp.sum(taken[8 + 8 * a:16 + 8 * a], axis=0, keepdims=True) for a in range(1, 8)]
        counts.append(taken[72:80])
        cnt = jnp.concatenate(counts, axis=0)
        c0 = jnp.zeros((n_keys, tok), F32)
        for a in range(TOPK):
            c0 = jnp.where(rank0 == float(a), cnt[a:a + 1], c0)
        a_ref[h] = jnp.exp(s0 - t0[0:1]) / zsum
        c0_ref[h] = c0
        b_ref[h] = jnp.exp(s1 - t1[0:1]).astype(b_ref.dtype)
        r1_ref[h] = rank1.astype(r1_ref.dtype)
        return 0

    lax.fori_loop(0, a_ref.shape[0], head, 0)


def _topk(st):
    nhp, n_keys, n = st.shape
    nh = nhp // 2
    tok = TOPK_TOK
    fidx = jnp.asarray(np.broadcast_to(_cand_layout()[:, None], (80, tok)))
    ospec = pl.BlockSpec((nh, n_keys, tok), lambda i: (0, 0, i))
    return pl.pallas_call(
        _topk_body,
        out_shape=[jax.ShapeDtypeStruct((nh, n_keys, n), dt) for dt in (F32, F32, BF16, BF16)],
        grid=(n // tok,),
        in_specs=[pl.BlockSpec((nhp, n_keys, tok), lambda i: (0, 0, i)), _const_spec((80, tok))],
        out_specs=[ospec] * 4,
        compiler_params=_cparams(("parallel",), 32),
        name="peer_topk",
    )(st, fidx)


def _peer_gate_rows(act_ref, coef_ref, a_ref, c0_ref, b_ref, r1_ref, i, il):
    nh, n_grp, sub, tok = b_ref.shape
    n_keys = n_grp * sub
    rows = slice(il * n_keys, (il + 1) * n_keys)
    for lt in range(tok // LANES):
        lanes = slice(lt * LANES, (lt + 1) * LANES)
        gate = None
        for h in range(nh):
            cnt = jnp.broadcast_to(c0_ref[h, i:i + 1, lanes], (sub, LANES)).astype(BF16)[None]
            amp = jnp.broadcast_to(a_ref[h, i:i + 1, lanes], (sub, LANES)).astype(BF16)[None]
            b = b_ref[h, :, :, lanes]
            term = jnp.where(r1_ref[h, :, :, lanes] < cnt, b, jnp.zeros_like(b)) * amp
            gate = term if gate is None else gate + term
        coef_ref[rows, lanes] = gate.reshape(n_keys, LANES) * _gelu(act_ref[rows, lanes]).astype(BF16)


def _peer_body(h2t_ref, a_ref, c0_ref, b_ref, r1_ref, down0_ref, downa_ref, downb_ref, upta_ref, uptb_ref,
               x1_ref, gate2_ref, gf_ref, y_ref, acc_ref, acta_ref, actb_ref, coefa_ref, coefb_ref,
               bp_ref, r1p_ref, *, final_norm):
    g = pl.program_id(1)
    last = pl.num_programs(1) - 1
    side = (a_ref, c0_ref, bp_ref, r1p_ref)
    ech, d = downa_ref.shape
    ic = ech // (b_ref.shape[1] * b_ref.shape[2])
    piece = PEER_PIECE

    def act_piece(act_ref, down_ref, r):
        rows = slice(r * piece, (r + 1) * piece)
        act_ref[rows, :] = jnp.dot(down_ref[rows, :], h2t_ref[...], preferred_element_type=F32)

    def up_piece(upt_ref, coef_ref, r):
        rows = slice(r * piece, (r + 1) * piece)
        acc_ref[rows, :] += jnp.dot(upt_ref[rows, :], coef_ref[...], preferred_element_type=F32)

    def half(upt_ref, coef_in_ref, act_out_ref, down_ref, act_in_ref, coef_out_ref, which):
        work = ([functools.partial(up_piece, upt_ref, coef_in_ref, r) for r in range(d // piece)]
                + [functools.partial(act_piece, act_out_ref, down_ref, r) for r in range(ech // piece)])
        per_row = -(-len(work) // ic)
        for il in range(ic):
            for job in work[il * per_row:(il + 1) * per_row]:
                job()
            _peer_gate_rows(act_in_ref, coef_out_ref, *side, which * ic + il, il)

    @pl.when(g == 0)
    def _():
        acc_ref[...] = jnp.zeros_like(acc_ref)
        coefb_ref[...] = jnp.zeros_like(coefb_ref)
        bp_ref[...] = b_ref[...]
        r1p_ref[...] = r1_ref[...]
        acta_ref[...] = jnp.dot(down0_ref[...], h2t_ref[...], preferred_element_type=F32)

    @pl.when(g < last)
    def _():
        half(upta_ref, coefb_ref, actb_ref, downa_ref, acta_ref, coefa_ref, 0)
        half(uptb_ref, coefa_ref, acta_ref, downb_ref, actb_ref, coefb_ref, 1)

    @pl.when(g == last)
    def _():
        tb, tt, _ = x1_ref.shape
        out = acc_ref[...] + jnp.dot(upta_ref[...], coefb_ref[...], preferred_element_type=F32)
        y = x1_ref[...] + gate2_ref[...] * out.T.reshape(tb, tt, d)
        if final_norm:
            y = (y * lax.rsqrt(jnp.mean(y * y, axis=-1, keepdims=True) + EPS)) * gf_ref[...]
        y_ref[...] = y


def _peer(h2t, a, c0, b16, r116, down16, upt16, x1, gate2, g_final, tb, tt, final_norm):
    nb, t, d = x1.shape
    nh, n_keys, n = b16.shape
    tok = tb * tt
    sub = BF16_SUBLANES
    packed = lambda x: x.reshape(nh, n_keys // sub, sub, n)
    ne = down16.shape[0]
    ech = PEER_ECHUNK
    ic = ech // n_keys
    nc = ne // ech
    assert nc % 2 == 0
    per_tok = t // tt
    lane = lambda i, g: (0, 0, 0, i)
    side = pl.BlockSpec((nh, 2 * ic, tok), lambda i, g: (0, jnp.minimum(g, nc // 2 - 1), i))
    tokspec = pl.BlockSpec((tb, tt, d), lambda i, g: (i // per_tok, i % per_tok, 0))

    def down_spec(chunk):
        return pl.BlockSpec((ech, d), lambda i, g: (jnp.clip(chunk(g), 0, nc - 1), 0))

    def up_spec(chunk):
        return pl.BlockSpec((d, ech), lambda i, g: (0, jnp.clip(chunk(g), 0, nc - 1)))

    return pl.pallas_call(
        functools.partial(_peer_body, final_norm=final_norm),
        out_shape=jax.ShapeDtypeStruct((nb, t, d), F32),
        grid=(n // tok, nc // 2 + 1),
        in_specs=[pl.BlockSpec((d, tok), lambda i, g: (0, i)), side, side,
                  pl.BlockSpec((nh, n_keys // sub, sub, tok), lane),
                  pl.BlockSpec((nh, n_keys // sub, sub, tok), lane),
                  down_spec(lambda g: 0), down_spec(lambda g: 2 * g + 1), down_spec(lambda g: 2 * g + 2),
                  up_spec(lambda g: 2 * g - 1), up_spec(lambda g: 2 * g),
                  tokspec,
                  pl.BlockSpec((tb, 1, d), lambda i, g: (i // per_tok, 0, 0)),
                  pl.BlockSpec((1, d), lambda i, g: (0, 0))],
        out_specs=tokspec,
        scratch_shapes=[pltpu.VMEM((d, tok), F32), pltpu.VMEM((ech, tok), F32), pltpu.VMEM((ech, tok), F32),
                        pltpu.VMEM((ech, tok), BF16), pltpu.VMEM((ech, tok), BF16),
                        pltpu.VMEM((nh, n_keys // sub, sub, tok), BF16),
                        pltpu.VMEM((nh, n_keys // sub, sub, tok), BF16)],
        compiler_params=_cparams(("parallel", "arbitrary"), 56),
        name="peer",
    )(h2t, a, c0, packed(b16), packed(r116), down16, down16, down16, upt16, upt16, x1, gate2,
      g_final.reshape(1, d))


def _transpose_body(x_ref, o_ref):
    o_ref[...] = x_ref[...].T.astype(o_ref.dtype)


def _transpose_bf16(x):
    r, c = x.shape
    blk = 1024
    return pl.pallas_call(
        _transpose_body,
        out_shape=jax.ShapeDtypeStruct((c, r), BF16),
        grid=(r // blk, c // blk),
        in_specs=[pl.BlockSpec((blk, blk), lambda i, j: (i, j))],
        out_specs=pl.BlockSpec((blk, blk), lambda i, j: (j, i)),
        compiler_params=_cparams(("parallel", "parallel"), 32),
        name="transpose_cast",
    )(x)


def _row_tiles(nb, t):
    if t % ROW_TILE == 0:
        return 1, ROW_TILE
    assert ROW_TILE % t == 0 and t % 8 == 0 and nb % (ROW_TILE // t) == 0, (nb, t)
    return ROW_TILE // t, t


def _layer(x, mod, weights, attend, final_norm, g_final):
    (g1, w_in16, g_v, b_v, w_spatial, b_spatial, wba16, wbb16, wout16, g2, wq16, sk16, down16, upt16) = weights
    nb, t, d = x.shape
    tb, tt = _row_tiles(nb, t)
    shift1, scale1, gate1, shift2, scale2, gate2 = (mod[:, i:i + 1, :] for i in range(N_MOD))
    q16, k, k16, v, v16, u, gv, ga, gb = _inproj(x, shift1, scale1, g1, w_in16, g_v, b_v, tb, tt)
    attn = attend(q16, k, k16, v, v16)
    x1, h2 = _mix(u, gv, ga, gb, attn, x, gate1, shift2, scale2, g2, w_spatial, b_spatial,
                  wba16, wbb16, wout16, tb, tt)
    st, h2t = _scores(h2.reshape(nb * t, d), wq16, sk16)
    a, c0, b, r1 = _topk(st)
    ptb, ptt = (1, PEER_TOK) if t % PEER_TOK == 0 else (PEER_TOK // t, t)
    y = _peer(h2t, a, c0, b, r1, down16, upt16, x1, gate2, g_final, ptb, ptt, final_norm)
    return y, k, v, gv


def kernel(x_prompt, x_sample, cache_k, cache_v, page_table, c_prompt, c_sample, w_ada, b_ada, g_norm1, w_in,
           sb_bias, g_v, b_v, w_spatial, b_spatial, w_branch_a, w_branch_b, w_out, g_norm2, w_query, sub_keys,
           expert_down, expert_up, g_final):
    depth = w_ada.shape[0]
    n_prompt, n_sample = c_prompt.shape[0], c_sample.shape[0]
    d = x_prompt.shape[-1]
    hd = d // N_HEADS
    c_all = jnp.concatenate([c_prompt, c_sample], axis=0)
    n_c = c_all.shape[0]
    c_all = jnp.pad(c_all, ((0, (-n_c) % 16), (0, 0)))
    xp, xs = x_prompt, x_sample
    rows = [[] for _ in range(5)]
    for l in range(depth):
        mod = _adaln(c_all, w_ada[l], b_ada[l]).reshape(-1, N_MOD, d)
        weights = (g_norm1[l], w_in[l].astype(BF16), g_v[l], b_v[l], w_spatial[l], b_spatial[l],
                   w_branch_a[l].astype(BF16), w_branch_b[l].astype(BF16), w_out[l].astype(BF16), g_norm2[l],
                   w_query[l].astype(BF16), sub_keys[l].astype(BF16), expert_down[l].astype(BF16),
                   _transpose_bf16(expert_up[l]))
        last = l == depth - 1
        xp, kp, vp, _ = _layer(
            xp, mod[:n_prompt], weights,
            lambda q16, k, k16, v, v16: _attn_prompt(q16, k16, v16, sb_bias[l]), last, g_final)
        xs, ks, vs, gs = _layer(
            xs, mod[n_prompt:n_c], weights,
            lambda q16, k, k16, v, v16: _attn_sample(q16, k, v, cache_k[l], cache_v[l], page_table, sb_bias[l]),
            last, g_final)
        for lst, val in zip(rows, (kp, vp, ks, vs, gs)):
            lst.append(val.reshape(val.shape[0], val.shape[1], N_HEADS, hd))
    if depth == 0:
        raise ValueError("depth must be positive")
    return (xp, xs) + tuple(jnp.stack(r) for r in rows)
```

```python
import functools
import math

import numpy as np
import jax
import jax.numpy as jnp
from jax import lax
from jax.experimental import pallas as pl
from jax.experimental.pallas import tpu as pltpu

F32 = jnp.float32
BF16 = jnp.bfloat16

N_HEADS = 8
GM_GROUPS = 8
CHUNK = 128
PEER_HEADS = 8
TOPK = 16
N_MOD = 6
N_SEC = 7
EPS = 1e-6
INV_SQRT2 = 0.7071067811865476
NEG_INF = float("-inf")

ROW_TILE = 256
ATT_BLOCK = 256
ATT_HEADS = 2
ATT_PAGES = 8
PEER_TOK = 512
LANES = 128
PEER_ECHUNK = 1024
PEER_ROW_GROUP = 1
PEER_ACT_PIECES = 4
BF16_SUBLANES = 16
TOPK_TOK = 128
MIB = 1024 * 1024


def _cparams(semantics, vmem_mib):
    return pltpu.CompilerParams(dimension_semantics=semantics, vmem_limit_bytes=vmem_mib * MIB)


def _const_spec(shape):
    nd = len(shape)
    return pl.BlockSpec(shape, lambda *_: (0,) * nd, pipeline_mode=pl.Buffered(1))


def _gelu(x):
    return 0.5 * x * (1.0 + lax.erf(x * INV_SQRT2))


def _rms_mod(x, g, scale, shift):
    y = x * lax.rsqrt(jnp.mean(x * x, axis=-1, keepdims=True) + EPS)
    return (y * g) * (1.0 + scale) + shift


def _ada_body(c_ref, w_ref, b_ref, o_ref):
    c = c_ref[...]
    s = c * jax.nn.sigmoid(c)
    o_ref[...] = jnp.dot(s.astype(BF16), w_ref[...].astype(BF16), preferred_element_type=F32) + b_ref[...]


def _adaln(c_all, w_ada, b_ada):
    nb, d = c_all.shape
    return pl.pallas_call(
        _ada_body,
        out_shape=jax.ShapeDtypeStruct((nb, N_MOD * d), F32),
        grid=(N_MOD,),
        in_specs=[
            pl.BlockSpec((nb, d), lambda j: (0, 0)),
            pl.BlockSpec((d, d), lambda j: (0, j)),
            pl.BlockSpec((1, d), lambda j: (0, j)),
        ],
        out_specs=pl.BlockSpec((nb, d), lambda j: (0, j)),
        compiler_params=_cparams(("arbitrary",), 32),
        name="adaln",
    )(c_all, w_ada, b_ada.reshape(1, -1))


def _inproj_body(x_ref, shift_ref, scale_ref, g1_ref, w_ref, gvg_ref, gvb_ref,
                 q16_ref, k_ref, k16_ref, v_ref, v16_ref, u_ref, gv_ref, ga_ref, gb_ref):
    tb, tt, d = x_ref.shape
    h = _rms_mod(x_ref[...], g1_ref[...], scale_ref[...], shift_ref[...])
    h16 = h.reshape(tb * tt, d).astype(BF16)

    def proj(sec):
        return jnp.dot(h16, w_ref[:, sec * d:(sec + 1) * d], preferred_element_type=F32)

    def put(ref, val):
        ref[...] = val.reshape(tb, tt, d).astype(ref.dtype)

    put(q16_ref, proj(0))
    k = proj(1)
    put(k_ref, k)
    put(k16_ref, k)
    v = proj(2)
    put(v_ref, v)
    put(v16_ref, v)
    put(u_ref, _gelu(proj(3)))
    g = _gelu(proj(4))
    mu = jnp.mean(g, axis=-1, keepdims=True)
    gc = g - mu
    var = jnp.mean(gc * gc, axis=-1, keepdims=True)
    put(gv_ref, (gc * lax.rsqrt(var + EPS)) * gvg_ref[...] + gvb_ref[...])
    put(ga_ref, jax.nn.sigmoid(proj(5)))
    put(gb_ref, jax.nn.sigmoid(proj(6)))


def _inproj(x, shift, scale, g1, w_in16, g_v, b_v, tb, tt):
    nb, t, d = x.shape
    grid = (nb // tb, t // tt)
    tok = pl.BlockSpec((tb, tt, d), lambda i, j: (i, j, 0))
    mod = pl.BlockSpec((tb, 1, d), lambda i, j: (i, 0, 0))
    outs = [BF16, F32, BF16, F32, BF16, F32, F32, F32, F32]
    return pl.pallas_call(
        _inproj_body,
        out_shape=[jax.ShapeDtypeStruct((nb, t, d), dt) for dt in outs],
        grid=grid,
        in_specs=[tok, mod, mod, _const_spec((1, d)), _const_spec(w_in16.shape),
                  _const_spec((1, d)), _const_spec((1, d))],
        out_specs=[tok] * len(outs),
        compiler_params=_cparams(("parallel", "parallel"), 56),
        name="inproj",
    )(x, shift, scale, g1.reshape(1, d), w_in16, g_v.reshape(1, d), b_v.reshape(1, d))


def _sb_weights(z, readable, carry, tri):
    lk, la, total = _sb_log_keep(z, readable, tri)
    return _sb_finish(z, lk, la + carry, readable), carry + total


def _sb_log_keep(z, readable, tri):
    lk = -(jnp.maximum(z, 0.0) + jnp.log(1.0 + jnp.exp(-jnp.abs(z))))
    if readable is not None:
        lk = jnp.where(readable, lk, 0.0)
    la = jnp.dot(lk.astype(BF16), tri, preferred_element_type=F32)
    return lk, la, jnp.sum(lk, axis=1, keepdims=True)


def _sb_finish(z, lk, log_after, readable):
    w = jnp.exp(z + lk + log_after)
    if readable is not None:
        w = jnp.where(readable, w, 0.0)
    return w


def _attn_prompt_body(q_ref, k_ref, v_ref, tri_ref, bias_ref, o_ref, acc_ref, z_ref, w_ref, *, scale, hd):
    qi = pl.program_id(2)
    tq = q_ref.shape[1]
    nh = q_ref.shape[2] // hd
    tri = tri_ref[...]
    lanes = [slice(hh * hd, (hh + 1) * hd) for hh in range(nh)]

    def key_rows(kb):
        return pl.ds(pl.multiple_of(kb * tq, tq), tq)

    def put_logits(kb):
        rows = key_rows(kb)
        for hh in range(nh):
            z_ref[hh] = lax.dot_general(q_ref[0, :, lanes[hh]], k_ref[0, rows, lanes[hh]],
                                        (((1,), (1,)), ((), ())), preferred_element_type=F32)

    def add_values(kb):
        rows = key_rows(kb)
        for hh in range(nh):
            acc_ref[hh] += jnp.dot(w_ref[hh], v_ref[0, rows, lanes[hh]], preferred_element_type=F32)

    def put_weights(zs, readable, carries):
        out = []
        for hh in range(nh):
            w, carry = _sb_weights(zs[hh] * scale + bias_ref[0, hh], readable, carries[hh], tri)
            w_ref[hh] = w.astype(w_ref.dtype)
            out.append(carry)
        return tuple(out)

    acc_ref[...] = jnp.zeros_like(acc_ref)
    row = lax.broadcasted_iota(jnp.int32, (tq, tq), 0)
    col = lax.broadcasted_iota(jnp.int32, (tq, tq), 1)
    put_logits(qi)
    zs = [z_ref[hh] for hh in range(nh)]
    put_logits(jnp.maximum(qi - 1, 0))
    carries = put_weights(zs, col < row, [jnp.zeros((tq, 1), F32)] * nh)

    def step(i, carries):
        kb = qi - 1 - i
        zs = [z_ref[hh] for hh in range(nh)]
        add_values(kb + 1)
        put_logits(jnp.maximum(kb - 1, 0))
        return put_weights(zs, None, carries)

    lax.fori_loop(0, qi, step, carries)
    add_values(0)
    for hh in range(nh):
        o_ref[0, :, lanes[hh]] = acc_ref[hh].astype(o_ref.dtype)


def _tri(tk):
    return jnp.asarray(np.arange(tk)[:, None] > np.arange(tk)[None, :], dtype=BF16)


def _attn_prompt(q16, k16, v16, sb_bias):
    b, t, d = q16.shape
    hd = d // N_HEADS
    tq = ATT_BLOCK
    nh = ATT_HEADS
    bias = jnp.broadcast_to(sb_bias.astype(F32).reshape(N_HEADS // nh, nh, 1, 1), (N_HEADS // nh, nh, 1, tq))
    qspec = pl.BlockSpec((1, tq, nh * hd), lambda bi, h, qi: (bi, qi, h))
    kvspec = pl.BlockSpec((1, t, nh * hd), lambda bi, h, qi: (bi, 0, h))
    return pl.pallas_call(
        functools.partial(_attn_prompt_body, scale=hd ** -0.5, hd=hd),
        out_shape=jax.ShapeDtypeStruct((b, t, d), BF16),
        grid=(b, N_HEADS // nh, t // tq),
        in_specs=[qspec, kvspec, kvspec, _const_spec((tq, tq)),
                  pl.BlockSpec((1, nh, 1, tq), lambda bi, h, qi: (h, 0, 0, 0))],
        out_specs=qspec,
        scratch_shapes=[pltpu.VMEM((nh, tq, hd), F32), pltpu.VMEM((nh, tq, tq), F32),
                        pltpu.VMEM((nh, tq, tq), BF16)],
        compiler_params=_cparams(("parallel", "parallel", "arbitrary"), 32),
        name="attn_prompt",
    )(q16, k16, v16, _tri(tq), bias)


def _attn_sample_body(pt_ref, q_ref, kn_ref, vn_ref, *rest, scale, n_pp):
    del pt_ref
    kc_refs, vc_refs = rest[:n_pp], rest[n_pp:2 * n_pp]
    tri_ref, bias_ref, o_ref, carry_ref, acc_ref = rest[2 * n_pp:]
    p = pl.program_id(1)
    q16 = q_ref[0]
    rows, hd = q16.shape
    n_q = rows // N_HEADS
    lt = tri_ref.shape[1]
    tri = tri_ref[...]
    bias = bias_ref[...]
    row = lax.broadcasted_iota(jnp.int32, (rows, lt), 0)
    lane = lax.broadcasted_iota(jnp.int32, (rows, lt), 1)
    own_head = (lane % N_HEADS) == (row // n_q)

    def blocks(kvs, readable):
        zs = [lax.dot_general(q16, k2, (((1,), (1,)), ((), ())), preferred_element_type=F32) for k2, _ in kvs]
        carry = carry_ref[...]
        tiles = []
        for z in zs:
            row_tiles = [None] * (z.shape[1] // lt)
            for t in reversed(range(len(row_tiles))):
                zt = z[:, t * lt:(t + 1) * lt] * scale + bias
                lk, la, total = _sb_log_keep(zt, readable, tri)
                row_tiles[t] = (zt, lk, la + carry)
                carry = carry + total
            tiles.append(row_tiles)
        carry_ref[...] = carry
        acc = acc_ref[...]
        for row_tiles, (_, v2) in zip(tiles, kvs):
            w = jnp.concatenate([_sb_finish(zt, lk, la, readable).astype(BF16) for zt, lk, la in row_tiles], axis=1)
            acc = acc + jnp.dot(w, v2, preferred_element_type=F32)
        acc_ref[...] = acc

    @pl.when(p == 0)
    def _():
        carry_ref[...] = jnp.zeros_like(carry_ref)
        acc_ref[...] = jnp.zeros_like(acc_ref)
        blocks([(kn_ref[0], vn_ref[0])], own_head & ((lane // N_HEADS) < (row % n_q)))

    def page_rows(ref):
        return ref[0].reshape(ref.shape[1] * ref.shape[2], hd).astype(BF16)

    blocks([(page_rows(kc_refs[j]), page_rows(vc_refs[j])) for j in range(n_pp)], own_head)

    @pl.when(p == pl.num_programs(1) - 1)
    def _():
        o_ref[0] = acc_ref[...]


def _attn_sample(q16, k_new, v_new, cache_k, cache_v, page_table, sb_bias):
    nb, n_new, d = q16.shape
    n_pages = page_table.shape[1]
    page, hd = cache_k.shape[1], cache_k.shape[3]
    n_pp = ATT_PAGES
    lt = 128
    rows = N_HEADS * n_new
    assert n_pages % n_pp == 0 and (N_HEADS * n_new) <= lt and lt % N_HEADS == 0

    def head_major(x, dtype):
        return x.reshape(nb, n_new, N_HEADS, hd).transpose(0, 2, 1, 3).reshape(nb, rows, hd).astype(dtype)

    def lane_rows(x):
        return jnp.pad(x.reshape(nb, n_new * N_HEADS, hd), ((0, 0), (0, lt - n_new * N_HEADS), (0, 0))).astype(BF16)

    bias = jnp.broadcast_to(jnp.repeat(sb_bias.astype(F32), n_new)[:, None], (rows, lt))
    row_spec = pl.BlockSpec((1, rows, hd), lambda b, p, pt: (b, 0, 0))
    new_spec = pl.BlockSpec((1, lt, hd), lambda b, p, pt: (b, 0, 0))

    def page_spec(j):
        return pl.BlockSpec((1, page, N_HEADS, hd),
                            lambda b, p, pt: (pt[b, n_pages - 1 - (p * n_pp + j)], 0, 0, 0))

    grid_spec = pltpu.PrefetchScalarGridSpec(
        num_scalar_prefetch=1,
        grid=(nb, n_pages // n_pp),
        in_specs=[row_spec, new_spec, new_spec] + [page_spec(j) for j in range(n_pp)] * 2
        + [pl.BlockSpec((lt, lt), lambda b, p, pt: (0, 0)), pl.BlockSpec((rows, lt), lambda b, p, pt: (0, 0))],
        out_specs=row_spec,
        scratch_shapes=[pltpu.VMEM((rows, 1), F32), pltpu.VMEM((rows, hd), F32)],
    )
    out = pl.pallas_call(
        functools.partial(_attn_sample_body, scale=hd ** -0.5, n_pp=n_pp),
        out_shape=jax.ShapeDtypeStruct((nb, rows, hd), F32),
        grid_spec=grid_spec,
        compiler_params=_cparams(("parallel", "arbitrary"), 40),
        name="attn_sample",
    )(page_table, head_major(q16, BF16), lane_rows(k_new), lane_rows(v_new),
      *([cache_k] * n_pp), *([cache_v] * n_pp), _tri(lt), bias)
    return out.reshape(nb, N_HEADS, n_new, hd).transpose(0, 2, 1, 3).reshape(nb, n_new, d)


def _spatial_gate_chunks(gv, w_ref, bs_ref):
    r, d = gv.shape
    gc = d // GM_GROUPS
    row = lax.broadcasted_iota(jnp.int32, (CHUNK, CHUNK), 0)
    col = lax.broadcasted_iota(jnp.int32, (CHUNK, CHUNK), 1)
    gv16 = gv.astype(BF16)
    cols = []
    for g in range(GM_GROUPS):
        w = jnp.where(col <= row, w_ref[g], 0.0).astype(BF16)
        parts = [jnp.dot(w, gv16[c * CHUNK:(c + 1) * CHUNK, g * gc:(g + 1) * gc], preferred_element_type=F32)
                 + bs_ref[:, g:g + 1] for c in range(r // CHUNK)]
        cols.append(jnp.concatenate(parts, axis=0))
    return jnp.concatenate(cols, axis=1)


def _spatial_gate_partial(gv3, wc_ref, bs_ref):
    tb, tt, d = gv3.shape
    gc = d // GM_GROUPS
    t_idx = lax.broadcasted_iota(jnp.int32, (tt, 1), 0)
    cols = []
    for g in range(GM_GROUPS):
        gvg = gv3[:, :, g * gc:(g + 1) * gc]
        mixed = jnp.broadcast_to(bs_ref[:, g:g + 1][None], (tb, tt, gc))
        for s in range(tt):
            wcol = jnp.where(t_idx >= s, wc_ref[g, s], 0.0)
            mixed = mixed + wcol[None] * gvg[:, s:s + 1, :]
        cols.append(mixed)
    return jnp.concatenate(cols, axis=2).reshape(tb * tt, d)


def _mix_body(u_ref, gv_ref, ga_ref, gb_ref, at_ref, x_ref, gate1_ref, shift2_ref, scale2_ref, g2_ref,
              wsp_ref, bs_ref, wba_ref, wbb_ref, wout_ref, x1_ref, h2_ref):
    tb, tt, d = x_ref.shape
    r = tb * tt
    if tt % CHUNK == 0:
        mixed = _spatial_gate_chunks(gv_ref[...].reshape(r, d), wsp_ref, bs_ref)
    else:
        mixed = _spatial_gate_partial(gv_ref[...], wsp_ref, bs_ref)
    out_a = (u_ref[...].reshape(r, d) * mixed).astype(BF16)
    out_b = at_ref[...].reshape(r, d).astype(BF16)
    merged = (ga_ref[...].reshape(r, d) * jnp.dot(out_a, wba_ref[...], preferred_element_type=F32)
              + gb_ref[...].reshape(r, d) * jnp.dot(out_b, wbb_ref[...], preferred_element_type=F32))
    mix = jnp.dot(merged.astype(BF16), wout_ref[...], preferred_element_type=F32).reshape(tb, tt, d)
    x1 = x_ref[...] + gate1_ref[...] * mix
    x1_ref[...] = x1
    h2_ref[...] = _rms_mod(x1, g2_ref[...], scale2_ref[...], shift2_ref[...]).astype(h2_ref.dtype)


def _mix(u, gv, ga, gb, attn, x, gate1, shift2, scale2, g2, w_spatial, b_spatial, wba16, wbb16, wout16, tb, tt):
    nb, t, d = x.shape
    tok = pl.BlockSpec((tb, tt, d), lambda i, j: (i, j, 0))
    mod = pl.BlockSpec((tb, 1, d), lambda i, j: (i, 0, 0))
    if tt % CHUNK == 0:
        wsp = w_spatial
        bs = b_spatial.T
    else:
        wsp = jnp.swapaxes(w_spatial[:, :tt, :tt], 1, 2)[..., None]
        bs = b_spatial.T[:tt]
    return pl.pallas_call(
        _mix_body,
        out_shape=[jax.ShapeDtypeStruct((nb, t, d), F32), jax.ShapeDtypeStruct((nb, t, d), BF16)],
        grid=(nb // tb, t // tt),
        in_specs=[tok] * 6 + [mod] * 3 + [_const_spec((1, d)), _const_spec(wsp.shape), _const_spec(bs.shape),
                                          _const_spec(wba16.shape), _const_spec(wbb16.shape),
                                          _const_spec(wout16.shape)],
        out_specs=[tok, tok],
        compiler_params=_cparams(("parallel", "parallel"), 48),
        name="mix",
    )(u, gv, ga, gb, attn, x, gate1, shift2, scale2, g2.reshape(1, d), wsp, bs, wba16, wbb16, wout16)


def _scores_body(h2_ref, wq_ref, sk_ref, st_ref, h2t_ref):
    h2 = h2_ref[...]
    nsk = sk_ref.shape[2]
    q16 = jnp.dot(h2, wq_ref[...], preferred_element_type=F32).astype(BF16)
    for hp in range(st_ref.shape[0]):
        st_ref[hp] = lax.dot_general(sk_ref[hp % 2], q16[:, hp * nsk:(hp + 1) * nsk],
                                     (((1,), (1,)), ((), ())), preferred_element_type=F32)
    h2t_ref[...] = h2.astype(F32).T.astype(h2t_ref.dtype)


def _scores(h2, wq16, sk16):
    n, d = h2.shape
    n_keys, dsk = sk16.shape[1], sk16.shape[2]
    nhp = wq16.shape[1] // dsk
    r = ROW_TILE
    return pl.pallas_call(
        _scores_body,
        out_shape=[jax.ShapeDtypeStruct((nhp, n_keys, n), F32), jax.ShapeDtypeStruct((d, n), BF16)],
        grid=(n // r,),
        in_specs=[pl.BlockSpec((r, d), lambda i: (i, 0)), _const_spec(wq16.shape), _const_spec(sk16.shape)],
        out_specs=[pl.BlockSpec((nhp, n_keys, r), lambda i: (0, 0, i)), pl.BlockSpec((d, r), lambda i: (0, i))],
        compiler_params=_cparams(("parallel",), 32),
        name="peer_scores",
    )(h2, wq16, sk16)


def _top16(x, iota, by_index):
    n = float(x.shape[0])
    rank = jnp.full(x.shape, float(TOPK), F32)
    vals = []
    for r in range(TOPK):
        m = jnp.max(x, axis=0, keepdims=True)
        sel = x == m
        if by_index:
            sel = iota == jnp.min(jnp.where(sel, iota, n), axis=0, keepdims=True)
        rank = jnp.where(sel, float(r), rank)
        x = jnp.where(sel, NEG_INF, x)
        vals.append(m)
    taken = jnp.sum(jnp.where(rank < float(TOPK), 1.0, 0.0), axis=0, keepdims=True)
    return jnp.concatenate(vals, axis=0), rank, taken


def _cand_layout():
    groups = [(0, list(range(16)))] + [(a, list(range(8))) for a in range(1, 8)]
    fidx = []
    for a, bs in groups:
        fidx += [a * TOPK + b if (a + 1) * (b + 1) <= TOPK else -1 for b in bs]
    fidx += [a * TOPK for a in range(8, 16)]
    return np.asarray(fidx, np.float32)


def _topk_body(st_ref, fidx_ref, a_ref, c0_ref, b_ref, r1_ref):
    n_keys, tok = st_ref.shape[1], st_ref.shape[2]
    iota = lax.broadcasted_iota(jnp.int32, (n_keys, tok), 0).astype(F32)
    fidx = fidx_ref[...]
    big = float(TOPK * TOPK)

    def select(h, by_index):
        s0 = st_ref[2 * h]
        s1 = st_ref[2 * h + 1]
        t0, rank0, n0 = _top16(s0, iota, by_index)
        t1, rank1, n1 = _top16(s1, iota, by_index)
        rows = [t0[0:1] + t1]
        rows += [t0[a:a + 1] + t1[0:8] for a in range(1, 8)]
        rows += [t0[8:16] + t1[0:1]]
        cand = jnp.where(fidx >= 0, jnp.concatenate(rows, axis=0), NEG_INF)
        taken = jnp.zeros(cand.shape, F32)
        zsum = jnp.zeros((1, tok), F32)
        top = t0[0:1] + t1[0:1]
        for _r in range(TOPK):
            m = jnp.max(cand, axis=0, keepdims=True)
            sel = cand == m
            if by_index:
                sel = fidx == jnp.min(jnp.where(sel, fidx, big), axis=0, keepdims=True)
            taken = jnp.where(sel, 1.0, taken)
            cand = jnp.where(sel, NEG_INF, cand)
            zsum = zsum + jnp.exp(m - top)
        counts = [jnp.sum(taken[0:16], axis=0, keepdims=True)]
        counts += [jnp.sum(taken[8 + 8 * a:16 + 8 * a], axis=0, keepdims=True) for a in range(1, 8)]
        counts.append(taken[72:80])
        cnt = jnp.concatenate(counts, axis=0)
        c0 = jnp.zeros((n_keys, tok), F32)
        for a in range(TOPK):
            c0 = jnp.where(rank0 == float(a), cnt[a:a + 1], c0)
        a_ref[h] = jnp.exp(s0 - t0[0:1]) / zsum
        c0_ref[h] = c0
        b_ref[h] = jnp.exp(s1 - t1[0:1]).astype(b_ref.dtype)
        r1_ref[h] = rank1.astype(r1_ref.dtype)
        excess = jnp.maximum(jnp.maximum(n0, n1), jnp.sum(cnt, axis=0, keepdims=True)) - float(TOPK)
        return jnp.max(excess)

    def head(h, _):
        tied = select(h, False) > 0.0

        @pl.when(tied)
        def _():
            select(h, True)

        return 0

    lax.fori_loop(0, a_ref.shape[0], head, 0)


def _topk(st):
    nhp, n_keys, n = st.shape
    nh = nhp // 2
    tok = TOPK_TOK
    fidx = jnp.asarray(np.broadcast_to(_cand_layout()[:, None], (80, tok)))
    ospec = pl.BlockSpec((nh, n_keys, tok), lambda i: (0, 0, i))
    return pl.pallas_call(
        _topk_body,
        out_shape=[jax.ShapeDtypeStruct((nh, n_keys, n), dt) for dt in (F32, F32, BF16, BF16)],
        grid=(n // tok,),
        in_specs=[pl.BlockSpec((nhp, n_keys, tok), lambda i: (0, 0, i)), _const_spec((80, tok))],
        out_specs=[ospec] * 4,
        compiler_params=_cparams(("parallel",), 32),
        name="peer_topk",
    )(st, fidx)


def _peer_gate_rows(act_ref, coef_ref, a_ref, c0_ref, b_ref, r1_ref, first, count):
    nh, n_grp, sub, tok = b_ref.shape
    n_keys = n_grp * sub
    for lt in range(tok // LANES):
        lanes = slice(lt * LANES, (lt + 1) * LANES)
        gates = [None] * count
        for h in range(nh):
            b = b_ref[h, :, :, lanes]
            r1 = r1_ref[h, :, :, lanes]
            for k in range(count):
                i = first + k
                cnt = jnp.broadcast_to(c0_ref[h, i:i + 1, lanes], (sub, LANES)).astype(BF16)[None]
                amp = jnp.broadcast_to(a_ref[h, i:i + 1, lanes], (sub, LANES)).astype(BF16)[None]
                term = jnp.where(r1 < cnt, b, jnp.zeros_like(b)) * amp
                gates[k] = term if gates[k] is None else gates[k] + term
        for k in range(count):
            rows = slice((first + k) * n_keys, (first + k + 1) * n_keys)
            coef_ref[rows, lanes] = gates[k].reshape(n_keys, LANES) * _gelu(act_ref[rows, lanes]).astype(BF16)


def _peer_body(h2t_ref, a_ref, c0_ref, b_ref, r1_ref, down_ref, upt_ref, x1_ref, gate2_ref, gf_ref,
               y_ref, acc_ref, act_ref, coef_ref, bp_ref, r1p_ref, *, final_norm):
    c = pl.program_id(1)
    ic = act_ref.shape[0] // (b_ref.shape[1] * b_ref.shape[2])

    @pl.when(c == 0)
    def _():
        acc_ref[...] = jnp.zeros_like(acc_ref)
        bp_ref[...] = b_ref[...]
        r1p_ref[...] = r1_ref[...]

    piece = act_ref.shape[0] // PEER_ACT_PIECES
    for r in range(PEER_ACT_PIECES):
        rows = slice(r * piece, (r + 1) * piece)
        act_ref[rows, :] = jnp.dot(down_ref[rows, :], h2t_ref[...], preferred_element_type=F32)
    for first in range(0, ic, PEER_ROW_GROUP):
        _peer_gate_rows(act_ref, coef_ref, a_ref, c0_ref, bp_ref, r1p_ref, first, PEER_ROW_GROUP)
    acc_ref[...] += jnp.dot(upt_ref[...], coef_ref[...], preferred_element_type=F32)

    @pl.when(c == pl.num_programs(1) - 1)
    def _():
        tb, tt, d = x1_ref.shape
        y = x1_ref[...] + gate2_ref[...] * acc_ref[...].T.reshape(tb, tt, d)
        if final_norm:
            y = (y * lax.rsqrt(jnp.mean(y * y, axis=-1, keepdims=True) + EPS)) * gf_ref[...]
        y_ref[...] = y


def _peer(h2t, a, c0, b16, r116, down16, upt16, x1, gate2, g_final, tb, tt, final_norm):
    nb, t, d = x1.shape
    nh, n_keys, n = b16.shape
    tok = tb * tt
    sub = BF16_SUBLANES
    packed = lambda x: x.reshape(nh, n_keys // sub, sub, n)
    ne = down16.shape[0]
    ech = PEER_ECHUNK
    ic = ech // n_keys
    per_tok = t // tt
    lane = pl.BlockSpec((nh, n_keys // sub, sub, tok), lambda i, c: (0, 0, 0, i))
    side = pl.BlockSpec((nh, ic, tok), lambda i, c: (0, c, i))
    tokspec = pl.BlockSpec((tb, tt, d), lambda i, c: (i // per_tok, i % per_tok, 0))
    return pl.pallas_call(
        functools.partial(_peer_body, final_norm=final_norm),
        out_shape=jax.ShapeDtypeStruct((nb, t, d), F32),
        grid=(n // tok, ne // ech),
        in_specs=[pl.BlockSpec((d, tok), lambda i, c: (0, i)), side, side, lane, lane,
                  pl.BlockSpec((ech, d), lambda i, c: (c, 0)),
                  pl.BlockSpec((d, ech), lambda i, c: (0, c)),
                  tokspec,
                  pl.BlockSpec((tb, 1, d), lambda i, c: (i // per_tok, 0, 0)),
                  pl.BlockSpec((1, d), lambda i, c: (0, 0))],
        out_specs=tokspec,
        scratch_shapes=[pltpu.VMEM((d, tok), F32), pltpu.VMEM((ech, tok), F32), pltpu.VMEM((ech, tok), BF16),
                        pltpu.VMEM((nh, n_keys // sub, sub, tok), BF16),
                        pltpu.VMEM((nh, n_keys // sub, sub, tok), BF16)],
        compiler_params=_cparams(("parallel", "arbitrary"), 48),
        name="peer",
    )(h2t, a, c0, packed(b16), packed(r116), down16, upt16, x1, gate2, g_final.reshape(1, d))


def _transpose_body(x_ref, o_ref):
    o_ref[...] = x_ref[...].T.astype(o_ref.dtype)


def _transpose_bf16(x):
    r, c = x.shape
    blk = 1024
    return pl.pallas_call(
        _transpose_body,
        out_shape=jax.ShapeDtypeStruct((c, r), BF16),
        grid=(r // blk, c // blk),
        in_specs=[pl.BlockSpec((blk, blk), lambda i, j: (i, j))],
        out_specs=pl.BlockSpec((blk, blk), lambda i, j: (j, i)),
        compiler_params=_cparams(("parallel", "parallel"), 32),
        name="transpose_cast",
    )(x)


def _row_tiles(nb, t):
    if t % ROW_TILE == 0:
        return 1, ROW_TILE
    assert ROW_TILE % t == 0 and t % 8 == 0 and nb % (ROW_TILE // t) == 0, (nb, t)
    return ROW_TILE // t, t


def _layer(x, mod, weights, attend, final_norm, g_final):
    (g1, w_in16, g_v, b_v, w_spatial, b_spatial, wba16, wbb16, wout16, g2, wq16, sk16, down16, upt16) = weights
    nb, t, d = x.shape
    tb, tt = _row_tiles(nb, t)
    shift1, scale1, gate1, shift2, scale2, gate2 = (mod[:, i:i + 1, :] for i in range(N_MOD))
    q16, k, k16, v, v16, u, gv, ga, gb = _inproj(x, shift1, scale1, g1, w_in16, g_v, b_v, tb, tt)
    attn = attend(q16, k, k16, v, v16)
    x1, h2 = _mix(u, gv, ga, gb, attn, x, gate1, shift2, scale2, g2, w_spatial, b_spatial,
                  wba16, wbb16, wout16, tb, tt)
    st, h2t = _scores(h2.reshape(nb * t, d), wq16, sk16)
    a, c0, b, r1 = _topk(st)
    ptb, ptt = (1, PEER_TOK) if t % PEER_TOK == 0 else (PEER_TOK // t, t)
    y = _peer(h2t, a, c0, b, r1, down16, upt16, x1, gate2, g_final, ptb, ptt, final_norm)
    return y, k, v, gv


def kernel(x_prompt, x_sample, cache_k, cache_v, page_table, c_prompt, c_sample, w_ada, b_ada, g_norm1, w_in,
           sb_bias, g_v, b_v, w_spatial, b_spatial, w_branch_a, w_branch_b, w_out, g_norm2, w_query, sub_keys,
           expert_down, expert_up, g_final):
    depth = w_ada.shape[0]
    n_prompt, n_sample = c_prompt.shape[0], c_sample.shape[0]
    d = x_prompt.shape[-1]
    hd = d // N_HEADS
    c_all = jnp.concatenate([c_prompt, c_sample], axis=0)
    n_c = c_all.shape[0]
    c_all = jnp.pad(c_all, ((0, (-n_c) % 16), (0, 0)))
    xp, xs = x_prompt, x_sample
    rows = [[] for _ in range(5)]
    for l in range(depth):
        mod = _adaln(c_all, w_ada[l], b_ada[l]).reshape(-1, N_MOD, d)
        weights = (g_norm1[l], w_in[l].astype(BF16), g_v[l], b_v[l], w_spatial[l], b_spatial[l],
                   w_branch_a[l].astype(BF16), w_branch_b[l].astype(BF16), w_out[l].astype(BF16), g_norm2[l],
                   w_query[l].astype(BF16), sub_keys[l].astype(BF16), expert_down[l].astype(BF16),
                   _transpose_bf16(expert_up[l]))
        last = l == depth - 1
        xp, kp, vp, _ = _layer(
            xp, mod[:n_prompt], weights,
            lambda q16, k, k16, v, v16: _attn_prompt(q16, k16, v16, sb_bias[l]), last, g_final)
        xs, ks, vs, gs = _layer(
            xs, mod[n_prompt:n_c], weights,
            lambda q16, k, k16, v, v16: _attn_sample(q16, k, v, cache_k[l], cache_v[l], page_table, sb_bias[l]),
            last, g_final)
        for lst, val in zip(rows, (kp, vp, ks, vs, gs)):
            lst.append(val.reshape(val.shape[0], val.shape[1], N_HEADS, hd))
    if depth == 0:
        raise ValueError("depth must be positive")
    return (xp, xs) + tuple(jnp.stack(r) for r in rows)
```

```python
import functools
import math

import numpy as np
import jax
import jax.numpy as jnp
from jax import lax
from jax.experimental import pallas as pl
from jax.experimental.pallas import tpu as pltpu

F32 = jnp.float32
BF16 = jnp.bfloat16

N_HEADS = 8
GM_GROUPS = 8
CHUNK = 128
PEER_HEADS = 8
TOPK = 16
N_MOD = 6
N_SEC = 7
EPS = 1e-6
INV_SQRT2 = 0.7071067811865476
NEG_INF = float("-inf")

ROW_TILE = 256
ATT_BLOCK = 256
ATT_HEADS = 2
ATT_PAGES = 8
PEER_TOK = 256
LANES = 128
PEER_ECHUNK = 1024
PEER_ROW_GROUP = 1
PEER_ACT_PIECES = 4
BF16_SUBLANES = 16
TOPK_TOK = 128
MIB = 1024 * 1024


def _cparams(semantics, vmem_mib):
    return pltpu.CompilerParams(dimension_semantics=semantics, vmem_limit_bytes=vmem_mib * MIB)


def _const_spec(shape):
    nd = len(shape)
    return pl.BlockSpec(shape, lambda *_: (0,) * nd, pipeline_mode=pl.Buffered(1))


def _gelu(x):
    return 0.5 * x * (1.0 + lax.erf(x * INV_SQRT2))


def _rms_mod(x, g, scale, shift):
    y = x * lax.rsqrt(jnp.mean(x * x, axis=-1, keepdims=True) + EPS)
    return (y * g) * (1.0 + scale) + shift


def _ada_body(c_ref, w_ref, b_ref, o_ref):
    c = c_ref[...]
    s = c * jax.nn.sigmoid(c)
    o_ref[...] = jnp.dot(s.astype(BF16), w_ref[...].astype(BF16), preferred_element_type=F32) + b_ref[...]


def _adaln(c_all, w_ada, b_ada):
    nb, d = c_all.shape
    return pl.pallas_call(
        _ada_body,
        out_shape=jax.ShapeDtypeStruct((nb, N_MOD * d), F32),
        grid=(N_MOD,),
        in_specs=[
            pl.BlockSpec((nb, d), lambda j: (0, 0)),
            pl.BlockSpec((d, d), lambda j: (0, j)),
            pl.BlockSpec((1, d), lambda j: (0, j)),
        ],
        out_specs=pl.BlockSpec((nb, d), lambda j: (0, j)),
        compiler_params=_cparams(("arbitrary",), 32),
        name="adaln",
    )(c_all, w_ada, b_ada.reshape(1, -1))


def _inproj_body(x_ref, shift_ref, scale_ref, g1_ref, w_ref, gvg_ref, gvb_ref,
                 q16_ref, k_ref, k16_ref, v_ref, v16_ref, u_ref, gv_ref, ga_ref, gb_ref):
    tb, tt, d = x_ref.shape
    h = _rms_mod(x_ref[...], g1_ref[...], scale_ref[...], shift_ref[...])
    h16 = h.reshape(tb * tt, d).astype(BF16)

    def proj(sec):
        return jnp.dot(h16, w_ref[:, sec * d:(sec + 1) * d], preferred_element_type=F32)

    def put(ref, val):
        ref[...] = val.reshape(tb, tt, d).astype(ref.dtype)

    put(q16_ref, proj(0))
    k = proj(1)
    put(k_ref, k)
    put(k16_ref, k)
    v = proj(2)
    put(v_ref, v)
    put(v16_ref, v)
    put(u_ref, _gelu(proj(3)))
    g = _gelu(proj(4))
    mu = jnp.mean(g, axis=-1, keepdims=True)
    gc = g - mu
    var = jnp.mean(gc * gc, axis=-1, keepdims=True)
    put(gv_ref, (gc * lax.rsqrt(var + EPS)) * gvg_ref[...] + gvb_ref[...])
    put(ga_ref, jax.nn.sigmoid(proj(5)))
    put(gb_ref, jax.nn.sigmoid(proj(6)))


def _inproj(x, shift, scale, g1, w_in16, g_v, b_v, tb, tt):
    nb, t, d = x.shape
    grid = (nb // tb, t // tt)
    tok = pl.BlockSpec((tb, tt, d), lambda i, j: (i, j, 0))
    mod = pl.BlockSpec((tb, 1, d), lambda i, j: (i, 0, 0))
    outs = [BF16, F32, BF16, F32, BF16, F32, F32, F32, F32]
    return pl.pallas_call(
        _inproj_body,
        out_shape=[jax.ShapeDtypeStruct((nb, t, d), dt) for dt in outs],
        grid=grid,
        in_specs=[tok, mod, mod, _const_spec((1, d)), _const_spec(w_in16.shape),
                  _const_spec((1, d)), _const_spec((1, d))],
        out_specs=[tok] * len(outs),
        compiler_params=_cparams(("parallel", "parallel"), 56),
        name="inproj",
    )(x, shift, scale, g1.reshape(1, d), w_in16, g_v.reshape(1, d), b_v.reshape(1, d))


def _sb_weights(z, readable, carry, tri):
    lk, la, total = _sb_log_keep(z, readable, tri)
    return _sb_finish(z, lk, la + carry, readable), carry + total


def _sb_log_keep(z, readable, tri):
    lk = -(jnp.maximum(z, 0.0) + jnp.log(1.0 + jnp.exp(-jnp.abs(z))))
    if readable is not None:
        lk = jnp.where(readable, lk, 0.0)
    la = jnp.dot(lk.astype(BF16), tri, preferred_element_type=F32)
    return lk, la, jnp.sum(lk, axis=1, keepdims=True)


def _sb_finish(z, lk, log_after, readable):
    w = jnp.exp(z + lk + log_after)
    if readable is not None:
        w = jnp.where(readable, w, 0.0)
    return w


def _attn_prompt_body(q_ref, k_ref, v_ref, tri_ref, bias_ref, o_ref, acc_ref, z_ref, w_ref, *, scale, hd):
    qi = pl.program_id(2)
    tq = q_ref.shape[1]
    nh = q_ref.shape[2] // hd
    tri = tri_ref[...]
    lanes = [slice(hh * hd, (hh + 1) * hd) for hh in range(nh)]

    def key_rows(kb):
        return pl.ds(pl.multiple_of(kb * tq, tq), tq)

    def put_logits(kb):
        rows = key_rows(kb)
        for hh in range(nh):
            z_ref[hh] = lax.dot_general(q_ref[0, :, lanes[hh]], k_ref[0, rows, lanes[hh]],
                                        (((1,), (1,)), ((), ())), preferred_element_type=F32)

    def add_values(kb):
        rows = key_rows(kb)
        for hh in range(nh):
            acc_ref[hh] += jnp.dot(w_ref[hh], v_ref[0, rows, lanes[hh]], preferred_element_type=F32)

    def put_weights(zs, readable, carries):
        out = []
        for hh in range(nh):
            w, carry = _sb_weights(zs[hh] * scale + bias_ref[0, hh], readable, carries[hh], tri)
            w_ref[hh] = w.astype(w_ref.dtype)
            out.append(carry)
        return tuple(out)

    acc_ref[...] = jnp.zeros_like(acc_ref)
    row = lax.broadcasted_iota(jnp.int32, (tq, tq), 0)
    col = lax.broadcasted_iota(jnp.int32, (tq, tq), 1)
    put_logits(qi)
    zs = [z_ref[hh] for hh in range(nh)]
    put_logits(jnp.maximum(qi - 1, 0))
    carries = put_weights(zs, col < row, [jnp.zeros((tq, 1), F32)] * nh)

    def step(i, carries):
        kb = qi - 1 - i
        zs = [z_ref[hh] for hh in range(nh)]
        add_values(kb + 1)
        put_logits(jnp.maximum(kb - 1, 0))
        return put_weights(zs, None, carries)

    lax.fori_loop(0, qi, step, carries)
    add_values(0)
    for hh in range(nh):
        o_ref[0, :, lanes[hh]] = acc_ref[hh].astype(o_ref.dtype)


def _tri(tk):
    return jnp.asarray(np.arange(tk)[:, None] > np.arange(tk)[None, :], dtype=BF16)


def _attn_prompt(q16, k16, v16, sb_bias):
    b, t, d = q16.shape
    hd = d // N_HEADS
    tq = ATT_BLOCK
    nh = ATT_HEADS
    bias = jnp.broadcast_to(sb_bias.astype(F32).reshape(N_HEADS // nh, nh, 1, 1), (N_HEADS // nh, nh, 1, tq))
    qspec = pl.BlockSpec((1, tq, nh * hd), lambda bi, h, qi: (bi, qi, h))
    kvspec = pl.BlockSpec((1, t, nh * hd), lambda bi, h, qi: (bi, 0, h))
    return pl.pallas_call(
        functools.partial(_attn_prompt_body, scale=hd ** -0.5, hd=hd),
        out_shape=jax.ShapeDtypeStruct((b, t, d), BF16),
        grid=(b, N_HEADS // nh, t // tq),
        in_specs=[qspec, kvspec, kvspec, _const_spec((tq, tq)),
                  pl.BlockSpec((1, nh, 1, tq), lambda bi, h, qi: (h, 0, 0, 0))],
        out_specs=qspec,
        scratch_shapes=[pltpu.VMEM((nh, tq, hd), F32), pltpu.VMEM((nh, tq, tq), F32),
                        pltpu.VMEM((nh, tq, tq), BF16)],
        compiler_params=_cparams(("parallel", "parallel", "arbitrary"), 32),
        name="attn_prompt",
    )(q16, k16, v16, _tri(tq), bias)


def _attn_sample_body(pt_ref, q_ref, kn_ref, vn_ref, *rest, scale, n_pp):
    del pt_ref
    kc_refs, vc_refs = rest[:n_pp], rest[n_pp:2 * n_pp]
    tri_ref, bias_ref, o_ref, carry_ref, acc_ref = rest[2 * n_pp:]
    p = pl.program_id(1)
    q16 = q_ref[0]
    rows, hd = q16.shape
    n_q = rows // N_HEADS
    lt = tri_ref.shape[1]
    tri = tri_ref[...]
    bias = bias_ref[...]
    row = lax.broadcasted_iota(jnp.int32, (rows, lt), 0)
    lane = lax.broadcasted_iota(jnp.int32, (rows, lt), 1)
    own_head = (lane % N_HEADS) == (row // n_q)

    def blocks(kvs, readable):
        zs = [lax.dot_general(q16, k2, (((1,), (1,)), ((), ())), preferred_element_type=F32) for k2, _ in kvs]
        carry = carry_ref[...]
        tiles = []
        for z in zs:
            row_tiles = [None] * (z.shape[1] // lt)
            for t in reversed(range(len(row_tiles))):
                zt = z[:, t * lt:(t + 1) * lt] * scale + bias
                lk, la, total = _sb_log_keep(zt, readable, tri)
                row_tiles[t] = (zt, lk, la + carry)
                carry = carry + total
            tiles.append(row_tiles)
        carry_ref[...] = carry
        acc = acc_ref[...]
        for row_tiles, (_, v2) in zip(tiles, kvs):
            w = jnp.concatenate([_sb_finish(zt, lk, la, readable).astype(BF16) for zt, lk, la in row_tiles], axis=1)
            acc = acc + jnp.dot(w, v2, preferred_element_type=F32)
        acc_ref[...] = acc

    @pl.when(p == 0)
    def _():
        carry_ref[...] = jnp.zeros_like(carry_ref)
        acc_ref[...] = jnp.zeros_like(acc_ref)
        blocks([(kn_ref[0], vn_ref[0])], own_head & ((lane // N_HEADS) < (row % n_q)))

    def page_rows(ref):
        return ref[0].reshape(ref.shape[1] * ref.shape[2], hd).astype(BF16)

    blocks([(page_rows(kc_refs[j]), page_rows(vc_refs[j])) for j in range(n_pp)], own_head)

    @pl.when(p == pl.num_programs(1) - 1)
    def _():
        o_ref[0] = acc_ref[...]


def _attn_sample(q16, k_new, v_new, cache_k, cache_v, page_table, sb_bias):
    nb, n_new, d = q16.shape
    n_pages = page_table.shape[1]
    page, hd = cache_k.shape[1], cache_k.shape[3]
    n_pp = ATT_PAGES
    lt = 128
    rows = N_HEADS * n_new
    assert n_pages % n_pp == 0 and (N_HEADS * n_new) <= lt and lt % N_HEADS == 0

    def head_major(x, dtype):
        return x.reshape(nb, n_new, N_HEADS, hd).transpose(0, 2, 1, 3).reshape(nb, rows, hd).astype(dtype)

    def lane_rows(x):
        return jnp.pad(x.reshape(nb, n_new * N_HEADS, hd), ((0, 0), (0, lt - n_new * N_HEADS), (0, 0))).astype(BF16)

    bias = jnp.broadcast_to(jnp.repeat(sb_bias.astype(F32), n_new)[:, None], (rows, lt))
    row_spec = pl.BlockSpec((1, rows, hd), lambda b, p, pt: (b, 0, 0))
    new_spec = pl.BlockSpec((1, lt, hd), lambda b, p, pt: (b, 0, 0))

    def page_spec(j):
        return pl.BlockSpec((1, page, N_HEADS, hd),
                            lambda b, p, pt: (pt[b, n_pages - 1 - (p * n_pp + j)], 0, 0, 0))

    grid_spec = pltpu.PrefetchScalarGridSpec(
        num_scalar_prefetch=1,
        grid=(nb, n_pages // n_pp),
        in_specs=[row_spec, new_spec, new_spec] + [page_spec(j) for j in range(n_pp)] * 2
        + [pl.BlockSpec((lt, lt), lambda b, p, pt: (0, 0)), pl.BlockSpec((rows, lt), lambda b, p, pt: (0, 0))],
        out_specs=row_spec,
        scratch_shapes=[pltpu.VMEM((rows, 1), F32), pltpu.VMEM((rows, hd), F32)],
    )
    out = pl.pallas_call(
        functools.partial(_attn_sample_body, scale=hd ** -0.5, n_pp=n_pp),
        out_shape=jax.ShapeDtypeStruct((nb, rows, hd), F32),
        grid_spec=grid_spec,
        compiler_params=_cparams(("parallel", "arbitrary"), 40),
        name="attn_sample",
    )(page_table, head_major(q16, BF16), lane_rows(k_new), lane_rows(v_new),
      *([cache_k] * n_pp), *([cache_v] * n_pp), _tri(lt), bias)
    return out.reshape(nb, N_HEADS, n_new, hd).transpose(0, 2, 1, 3).reshape(nb, n_new, d)


def _spatial_gate_chunks(gv, w_ref, bs_ref):
    r, d = gv.shape
    gc = d // GM_GROUPS
    row = lax.broadcasted_iota(jnp.int32, (CHUNK, CHUNK), 0)
    col = lax.broadcasted_iota(jnp.int32, (CHUNK, CHUNK), 1)
    gv16 = gv.astype(BF16)
    cols = []
    for g in range(GM_GROUPS):
        w = jnp.where(col <= row, w_ref[g], 0.0).astype(BF16)
        parts = [jnp.dot(w, gv16[c * CHUNK:(c + 1) * CHUNK, g * gc:(g + 1) * gc], preferred_element_type=F32)
                 + bs_ref[:, g:g + 1] for c in range(r // CHUNK)]
        cols.append(jnp.concatenate(parts, axis=0))
    return jnp.concatenate(cols, axis=1)


def _spatial_gate_partial(gv3, wc_ref, bs_ref):
    tb, tt, d = gv3.shape
    gc = d // GM_GROUPS
    t_idx = lax.broadcasted_iota(jnp.int32, (tt, 1), 0)
    cols = []
    for g in range(GM_GROUPS):
        gvg = gv3[:, :, g * gc:(g + 1) * gc]
        mixed = jnp.broadcast_to(bs_ref[:, g:g + 1][None], (tb, tt, gc))
        for s in range(tt):
            wcol = jnp.where(t_idx >= s, wc_ref[g, s], 0.0)
            mixed = mixed + wcol[None] * gvg[:, s:s + 1, :]
        cols.append(mixed)
    return jnp.concatenate(cols, axis=2).reshape(tb * tt, d)


def _mix_body(u_ref, gv_ref, ga_ref, gb_ref, at_ref, x_ref, gate1_ref, shift2_ref, scale2_ref, g2_ref,
              wsp_ref, bs_ref, wba_ref, wbb_ref, wout_ref, x1_ref, h2_ref):
    tb, tt, d = x_ref.shape
    r = tb * tt
    if tt % CHUNK == 0:
        mixed = _spatial_gate_chunks(gv_ref[...].reshape(r, d), wsp_ref, bs_ref)
    else:
        mixed = _spatial_gate_partial(gv_ref[...], wsp_ref, bs_ref)
    out_a = (u_ref[...].reshape(r, d) * mixed).astype(BF16)
    out_b = at_ref[...].reshape(r, d).astype(BF16)
    merged = (ga_ref[...].reshape(r, d) * jnp.dot(out_a, wba_ref[...], preferred_element_type=F32)
              + gb_ref[...].reshape(r, d) * jnp.dot(out_b, wbb_ref[...], preferred_element_type=F32))
    mix = jnp.dot(merged.astype(BF16), wout_ref[...], preferred_element_type=F32).reshape(tb, tt, d)
    x1 = x_ref[...] + gate1_ref[...] * mix
    x1_ref[...] = x1
    h2_ref[...] = _rms_mod(x1, g2_ref[...], scale2_ref[...], shift2_ref[...]).astype(h2_ref.dtype)


def _mix(u, gv, ga, gb, attn, x, gate1, shift2, scale2, g2, w_spatial, b_spatial, wba16, wbb16, wout16, tb, tt):
    nb, t, d = x.shape
    tok = pl.BlockSpec((tb, tt, d), lambda i, j: (i, j, 0))
    mod = pl.BlockSpec((tb, 1, d), lambda i, j: (i, 0, 0))
    if tt % CHUNK == 0:
        wsp = w_spatial
        bs = b_spatial.T
    else:
        wsp = jnp.swapaxes(w_spatial[:, :tt, :tt], 1, 2)[..., None]
        bs = b_spatial.T[:tt]
    return pl.pallas_call(
        _mix_body,
        out_shape=[jax.ShapeDtypeStruct((nb, t, d), F32), jax.ShapeDtypeStruct((nb, t, d), BF16)],
        grid=(nb // tb, t // tt),
        in_specs=[tok] * 6 + [mod] * 3 + [_const_spec((1, d)), _const_spec(wsp.shape), _const_spec(bs.shape),
                                          _const_spec(wba16.shape), _const_spec(wbb16.shape),
                                          _const_spec(wout16.shape)],
        out_specs=[tok, tok],
        compiler_params=_cparams(("parallel", "parallel"), 48),
        name="mix",
    )(u, gv, ga, gb, attn, x, gate1, shift2, scale2, g2.reshape(1, d), wsp, bs, wba16, wbb16, wout16)


def _scores_body(h2_ref, wq_ref, sk_ref, st_ref, h2t_ref):
    h2 = h2_ref[...]
    nsk = sk_ref.shape[2]
    q16 = jnp.dot(h2, wq_ref[...], preferred_element_type=F32).astype(BF16)
    for hp in range(st_ref.shape[0]):
        st_ref[hp] = lax.dot_general(sk_ref[hp % 2], q16[:, hp * nsk:(hp + 1) * nsk],
                                     (((1,), (1,)), ((), ())), preferred_element_type=F32)
    h2t_ref[...] = h2.astype(F32).T.astype(h2t_ref.dtype)


def _scores(h2, wq16, sk16):
    n, d = h2.shape
    n_keys, dsk = sk16.shape[1], sk16.shape[2]
    nhp = wq16.shape[1] // dsk
    r = ROW_TILE
    return pl.pallas_call(
        _scores_body,
        out_shape=[jax.ShapeDtypeStruct((nhp, n_keys, n), F32), jax.ShapeDtypeStruct((d, n), BF16)],
        grid=(n // r,),
        in_specs=[pl.BlockSpec((r, d), lambda i: (i, 0)), _const_spec(wq16.shape), _const_spec(sk16.shape)],
        out_specs=[pl.BlockSpec((nhp, n_keys, r), lambda i: (0, 0, i)), pl.BlockSpec((d, r), lambda i: (0, i))],
        compiler_params=_cparams(("parallel",), 32),
        name="peer_scores",
    )(h2, wq16, sk16)


def _top16(x, iota, by_index):
    n = float(x.shape[0])
    rank = jnp.full(x.shape, float(TOPK), F32)
    vals = []
    for r in range(TOPK):
        m = jnp.max(x, axis=0, keepdims=True)
        sel = x == m
        if by_index:
            sel = iota == jnp.min(jnp.where(sel, iota, n), axis=0, keepdims=True)
        rank = jnp.where(sel, float(r), rank)
        x = jnp.where(sel, NEG_INF, x)
        vals.append(m)
    taken = jnp.sum(jnp.where(rank < float(TOPK), 1.0, 0.0), axis=0, keepdims=True)
    return jnp.concatenate(vals, axis=0), rank, taken


def _cand_layout():
    groups = [(0, list(range(16)))] + [(a, list(range(8))) for a in range(1, 8)]
    fidx = []
    for a, bs in groups:
        fidx += [a * TOPK + b if (a + 1) * (b + 1) <= TOPK else -1 for b in bs]
    fidx += [a * TOPK for a in range(8, 16)]
    return np.asarray(fidx, np.float32)


def _topk_body(st_ref, fidx_ref, a_ref, c0_ref, b_ref, r1_ref):
    n_keys, tok = st_ref.shape[1], st_ref.shape[2]
    iota = lax.broadcasted_iota(jnp.int32, (n_keys, tok), 0).astype(F32)
    fidx = fidx_ref[...]
    big = float(TOPK * TOPK)

    def select(h, by_index):
        s0 = st_ref[2 * h]
        s1 = st_ref[2 * h + 1]
        t0, rank0, n0 = _top16(s0, iota, by_index)
        t1, rank1, n1 = _top16(s1, iota, by_index)
        rows = [t0[0:1] + t1]
        rows += [t0[a:a + 1] + t1[0:8] for a in range(1, 8)]
        rows += [t0[8:16] + t1[0:1]]
        cand = jnp.where(fidx >= 0, jnp.concatenate(rows, axis=0), NEG_INF)
        taken = jnp.zeros(cand.shape, F32)
        zsum = jnp.zeros((1, tok), F32)
        top = t0[0:1] + t1[0:1]
        for _r in range(TOPK):
            m = jnp.max(cand, axis=0, keepdims=True)
            sel = cand == m
            if by_index:
                sel = fidx == jnp.min(jnp.where(sel, fidx, big), axis=0, keepdims=True)
            taken = jnp.where(sel, 1.0, taken)
            cand = jnp.where(sel, NEG_INF, cand)
            zsum = zsum + jnp.exp(m - top)
        counts = [jnp.sum(taken[0:16], axis=0, keepdims=True)]
        counts += [jnp.sum(taken[8 + 8 * a:16 + 8 * a], axis=0, keepdims=True) for a in range(1, 8)]
        counts.append(taken[72:80])
        cnt = jnp.concatenate(counts, axis=0)
        c0 = jnp.zeros((n_keys, tok), F32)
        for a in range(TOPK):
            c0 = jnp.where(rank0 == float(a), cnt[a:a + 1], c0)
        a_ref[h] = jnp.exp(s0 - t0[0:1]) / zsum
        c0_ref[h] = c0
        b_ref[h] = jnp.exp(s1 - t1[0:1]).astype(b_ref.dtype)
        r1_ref[h] = rank1.astype(r1_ref.dtype)
        excess = jnp.maximum(jnp.maximum(n0, n1), jnp.sum(cnt, axis=0, keepdims=True)) - float(TOPK)
        return jnp.max(excess)

    def head(h, _):
        tied = select(h, False) > 0.0

        @pl.when(tied)
        def _():
            select(h, True)

        return 0

    lax.fori_loop(0, a_ref.shape[0], head, 0)


def _topk(st):
    nhp, n_keys, n = st.shape
    nh = nhp // 2
    tok = TOPK_TOK
    fidx = jnp.asarray(np.broadcast_to(_cand_layout()[:, None], (80, tok)))
    ospec = pl.BlockSpec((nh, n_keys, tok), lambda i: (0, 0, i))
    return pl.pallas_call(
        _topk_body,
        out_shape=[jax.ShapeDtypeStruct((nh, n_keys, n), dt) for dt in (F32, F32, BF16, BF16)],
        grid=(n // tok,),
        in_specs=[pl.BlockSpec((nhp, n_keys, tok), lambda i: (0, 0, i)), _const_spec((80, tok))],
        out_specs=[ospec] * 4,
        compiler_params=_cparams(("parallel",), 32),
        name="peer_topk",
    )(st, fidx)


def _peer_gate_rows(act_ref, coef_ref, a_ref, c0_ref, b_ref, r1_ref, first, count):
    nh, n_grp, sub, tok = b_ref.shape
    n_keys = n_grp * sub
    for lt in range(tok // LANES):
        lanes = slice(lt * LANES, (lt + 1) * LANES)
        gates = [None] * count
        for h in range(nh):
            b = b_ref[h, :, :, lanes]
            r1 = r1_ref[h, :, :, lanes]
            for k in range(count):
                i = first + k
                cnt = jnp.broadcast_to(c0_ref[h, i:i + 1, lanes], (sub, LANES)).astype(BF16)[None]
                amp = jnp.broadcast_to(a_ref[h, i:i + 1, lanes], (sub, LANES)).astype(BF16)[None]
                term = jnp.where(r1 < cnt, b, jnp.zeros_like(b)) * amp
                gates[k] = term if gates[k] is None else gates[k] + term
        for k in range(count):
            rows = slice((first + k) * n_keys, (first + k + 1) * n_keys)
            coef_ref[rows, lanes] = gates[k].reshape(n_keys, LANES) * _gelu(act_ref[rows, lanes]).astype(BF16)


def _peer_body(h2t_ref, a_ref, c0_ref, b_ref, r1_ref, down_ref, upt_ref, x1_ref, gate2_ref, gf_ref,
               y_ref, acc_ref, act_ref, coef_ref, bp_ref, r1p_ref, *, final_norm):
    c = pl.program_id(1)
    ic = act_ref.shape[0] // (b_ref.shape[1] * b_ref.shape[2])

    @pl.when(c == 0)
    def _():
        acc_ref[...] = jnp.zeros_like(acc_ref)
        bp_ref[...] = b_ref[...]
        r1p_ref[...] = r1_ref[...]

    piece = act_ref.shape[0] // PEER_ACT_PIECES
    for r in range(PEER_ACT_PIECES):
        rows = slice(r * piece, (r + 1) * piece)
        act_ref[rows, :] = jnp.dot(down_ref[rows, :], h2t_ref[...], preferred_element_type=F32)
    for first in range(0, ic, PEER_ROW_GROUP):
        _peer_gate_rows(act_ref, coef_ref, a_ref, c0_ref, bp_ref, r1p_ref, first, PEER_ROW_GROUP)
    acc_ref[...] += jnp.dot(upt_ref[...], coef_ref[...], preferred_element_type=F32)

    @pl.when(c == pl.num_programs(1) - 1)
    def _():
        tb, tt, d = x1_ref.shape
        y = x1_ref[...] + gate2_ref[...] * acc_ref[...].T.reshape(tb, tt, d)
        if final_norm:
            y = (y * lax.rsqrt(jnp.mean(y * y, axis=-1, keepdims=True) + EPS)) * gf_ref[...]
        y_ref[...] = y


def _peer(h2t, a, c0, b16, r116, down16, upt16, x1, gate2, g_final, tb, tt, final_norm):
    nb, t, d = x1.shape
    nh, n_keys, n = b16.shape
    tok = tb * tt
    sub = BF16_SUBLANES
    packed = lambda x: x.reshape(nh, n_keys // sub, sub, n)
    ne = down16.shape[0]
    ech = PEER_ECHUNK
    ic = ech // n_keys
    per_tok = t // tt
    lane = pl.BlockSpec((nh, n_keys // sub, sub, tok), lambda i, c: (0, 0, 0, i))
    side = pl.BlockSpec((nh, ic, tok), lambda i, c: (0, c, i))
    tokspec = pl.BlockSpec((tb, tt, d), lambda i, c: (i // per_tok, i % per_tok, 0))
    return pl.pallas_call(
        functools.partial(_peer_body, final_norm=final_norm),
        out_shape=jax.ShapeDtypeStruct((nb, t, d), F32),
        grid=(n // tok, ne // ech),
        in_specs=[pl.BlockSpec((d, tok), lambda i, c: (0, i)), side, side, lane, lane,
                  pl.BlockSpec((ech, d), lambda i, c: (c, 0)),
                  pl.BlockSpec((d, ech), lambda i, c: (0, c)),
                  tokspec,
                  pl.BlockSpec((tb, 1, d), lambda i, c: (i // per_tok, 0, 0)),
                  pl.BlockSpec((1, d), lambda i, c: (0, 0))],
        out_specs=tokspec,
        scratch_shapes=[pltpu.VMEM((d, tok), F32), pltpu.VMEM((ech, tok), F32), pltpu.VMEM((ech, tok), BF16),
                        pltpu.VMEM((nh, n_keys // sub, sub, tok), BF16),
                        pltpu.VMEM((nh, n_keys // sub, sub, tok), BF16)],
        compiler_params=_cparams(("parallel", "arbitrary"), 48),
        name="peer",
    )(h2t, a, c0, packed(b16), packed(r116), down16, upt16, x1, gate2, g_final.reshape(1, d))


def _transpose_body(x_ref, o_ref):
    o_ref[...] = x_ref[...].T.astype(o_ref.dtype)


def _transpose_bf16(x):
    r, c = x.shape
    blk = 1024
    return pl.pallas_call(
        _transpose_body,
        out_shape=jax.ShapeDtypeStruct((c, r), BF16),
        grid=(r // blk, c // blk),
        in_specs=[pl.BlockSpec((blk, blk), lambda i, j: (i, j))],
        out_specs=pl.BlockSpec((blk, blk), lambda i, j: (j, i)),
        compiler_params=_cparams(("parallel", "parallel"), 32),
        name="transpose_cast",
    )(x)


def _row_tiles(nb, t):
    if t % ROW_TILE == 0:
        return 1, ROW_TILE
    assert ROW_TILE % t == 0 and t % 8 == 0 and nb % (ROW_TILE // t) == 0, (nb, t)
    return ROW_TILE // t, t


def _layer(x, mod, weights, attend, final_norm, g_final):
    (g1, w_in16, g_v, b_v, w_spatial, b_spatial, wba16, wbb16, wout16, g2, wq16, sk16, down16, upt16) = weights
    nb, t, d = x.shape
    tb, tt = _row_tiles(nb, t)
    shift1, scale1, gate1, shift2, scale2, gate2 = (mod[:, i:i + 1, :] for i in range(N_MOD))
    q16, k, k16, v, v16, u, gv, ga, gb = _inproj(x, shift1, scale1, g1, w_in16, g_v, b_v, tb, tt)
    attn = attend(q16, k, k16, v, v16)
    x1, h2 = _mix(u, gv, ga, gb, attn, x, gate1, shift2, scale2, g2, w_spatial, b_spatial,
                  wba16, wbb16, wout16, tb, tt)
    st, h2t = _scores(h2.reshape(nb * t, d), wq16, sk16)
    a, c0, b, r1 = _topk(st)
    ptb, ptt = (1, PEER_TOK) if t % PEER_TOK == 0 else (PEER_TOK // t, t)
    y = _peer(h2t, a, c0, b, r1, down16, upt16, x1, gate2, g_final, ptb, ptt, final_norm)
    return y, k, v, gv


def kernel(x_prompt, x_sample, cache_k, cache_v, page_table, c_prompt, c_sample, w_ada, b_ada, g_norm1, w_in,
           sb_bias, g_v, b_v, w_spatial, b_spatial, w_branch_a, w_branch_b, w_out, g_norm2, w_query, sub_keys,
           expert_down, expert_up, g_final):
    depth = w_ada.shape[0]
    n_prompt, n_sample = c_prompt.shape[0], c_sample.shape[0]
    d = x_prompt.shape[-1]
    hd = d // N_HEADS
    c_all = jnp.concatenate([c_prompt, c_sample], axis=0)
    n_c = c_all.shape[0]
    c_all = jnp.pad(c_all, ((0, (-n_c) % 16), (0, 0)))
    xp, xs = x_prompt, x_sample
    rows = [[] for _ in range(5)]
    for l in range(depth):
        mod = _adaln(c_all, w_ada[l], b_ada[l]).reshape(-1, N_MOD, d)
        weights = (g_norm1[l], w_in[l].astype(BF16), g_v[l], b_v[l], w_spatial[l], b_spatial[l],
                   w_branch_a[l].astype(BF16), w_branch_b[l].astype(BF16), w_out[l].astype(BF16), g_norm2[l],
                   w_query[l].astype(BF16), sub_keys[l].astype(BF16), expert_down[l].astype(BF16),
                   _transpose_bf16(expert_up[l]))
        last = l == depth - 1
        xp, kp, vp, _ = _layer(
            xp, mod[:n_prompt], weights,
            lambda q16, k, k16, v, v16: _attn_prompt(q16, k16, v16, sb_bias[l]), last, g_final)
        xs, ks, vs, gs = _layer(
            xs, mod[n_prompt:n_c], weights,
            lambda q16, k, k16, v, v16: _attn_sample(q16, k, v, cache_k[l], cache_v[l], page_table, sb_bias[l]),
            last, g_final)
        for lst, val in zip(rows, (kp, vp, ks, vs, gs)):
            lst.append(val.reshape(val.shape[0], val.shape[1], N_HEADS, hd))
    if depth == 0:
        raise ValueError("depth must be positive")
    return (xp, xs) + tuple(jnp.stack(r) for r in rows)
```

```python
import functools
import math

import numpy as np
import jax
import jax.numpy as jnp
from jax import lax
from jax.experimental import pallas as pl
from jax.experimental.pallas import tpu as pltpu

F32 = jnp.float32
BF16 = jnp.bfloat16

N_HEADS = 8
GM_GROUPS = 8
CHUNK = 128
PEER_HEADS = 8
TOPK = 16
N_MOD = 6
N_SEC = 7
EPS = 1e-6
INV_SQRT2 = 0.7071067811865476
NEG_INF = float("-inf")

ROW_TILE = 256
ATT_BLOCK = 256
ATT_HEADS = 2
ATT_PAGES = 8
PEER_TOK = 512
LANES = 128
PEER_ECHUNK = 1024
PEER_ROW_GROUP = 1
PEER_ACT_PIECES = 4
BF16_SUBLANES = 16
TOPK_TOK = 128
MIB = 1024 * 1024


def _cparams(semantics, vmem_mib):
    return pltpu.CompilerParams(dimension_semantics=semantics, vmem_limit_bytes=vmem_mib * MIB)


def _const_spec(shape):
    nd = len(shape)
    return pl.BlockSpec(shape, lambda *_: (0,) * nd, pipeline_mode=pl.Buffered(1))


def _gelu(x):
    return 0.5 * x * (1.0 + lax.erf(x * INV_SQRT2))


def _rms_mod(x, g, scale, shift):
    y = x * lax.rsqrt(jnp.mean(x * x, axis=-1, keepdims=True) + EPS)
    return (y * g) * (1.0 + scale) + shift


def _ada_body(c_ref, w_ref, b_ref, o_ref):
    c = c_ref[...]
    s = c * jax.nn.sigmoid(c)
    o_ref[...] = jnp.dot(s.astype(BF16), w_ref[...].astype(BF16), preferred_element_type=F32) + b_ref[...]


def _adaln(c_all, w_ada, b_ada):
    nb, d = c_all.shape
    return pl.pallas_call(
        _ada_body,
        out_shape=jax.ShapeDtypeStruct((nb, N_MOD * d), F32),
        grid=(N_MOD,),
        in_specs=[
            pl.BlockSpec((nb, d), lambda j: (0, 0)),
            pl.BlockSpec((d, d), lambda j: (0, j)),
            pl.BlockSpec((1, d), lambda j: (0, j)),
        ],
        out_specs=pl.BlockSpec((nb, d), lambda j: (0, j)),
        compiler_params=_cparams(("arbitrary",), 32),
        name="adaln",
    )(c_all, w_ada, b_ada.reshape(1, -1))


def _inproj_body(x_ref, shift_ref, scale_ref, g1_ref, w_ref, gvg_ref, gvb_ref,
                 q16_ref, k_ref, k16_ref, v_ref, v16_ref, u_ref, gv_ref, ga_ref, gb_ref):
    tb, tt, d = x_ref.shape
    h = _rms_mod(x_ref[...], g1_ref[...], scale_ref[...], shift_ref[...])
    h16 = h.reshape(tb * tt, d).astype(BF16)

    def proj(sec):
        return jnp.dot(h16, w_ref[:, sec * d:(sec + 1) * d], preferred_element_type=F32)

    def put(ref, val):
        ref[...] = val.reshape(tb, tt, d).astype(ref.dtype)

    put(q16_ref, proj(0))
    k = proj(1)
    put(k_ref, k)
    put(k16_ref, k)
    v = proj(2)
    put(v_ref, v)
    put(v16_ref, v)
    put(u_ref, _gelu(proj(3)))
    g = _gelu(proj(4))
    mu = jnp.mean(g, axis=-1, keepdims=True)
    gc = g - mu
    var = jnp.mean(gc * gc, axis=-1, keepdims=True)
    put(gv_ref, (gc * lax.rsqrt(var + EPS)) * gvg_ref[...] + gvb_ref[...])
    put(ga_ref, jax.nn.sigmoid(proj(5)))
    put(gb_ref, jax.nn.sigmoid(proj(6)))


def _inproj(x, shift, scale, g1, w_in16, g_v, b_v, tb, tt):
    nb, t, d = x.shape
    grid = (nb // tb, t // tt)
    tok = pl.BlockSpec((tb, tt, d), lambda i, j: (i, j, 0))
    mod = pl.BlockSpec((tb, 1, d), lambda i, j: (i, 0, 0))
    outs = [BF16, F32, BF16, F32, BF16, F32, F32, F32, F32]
    return pl.pallas_call(
        _inproj_body,
        out_shape=[jax.ShapeDtypeStruct((nb, t, d), dt) for dt in outs],
        grid=grid,
        in_specs=[tok, mod, mod, _const_spec((1, d)), _const_spec(w_in16.shape),
                  _const_spec((1, d)), _const_spec((1, d))],
        out_specs=[tok] * len(outs),
        compiler_params=_cparams(("parallel", "parallel"), 56),
        name="inproj",
    )(x, shift, scale, g1.reshape(1, d), w_in16, g_v.reshape(1, d), b_v.reshape(1, d))


def _sb_weights(z, readable, carry, tri):
    lk, la, total = _sb_log_keep(z, readable, tri)
    return _sb_finish(z, lk, la + carry, readable), carry + total


def _sb_log_keep(z, readable, tri):
    lk = -(jnp.maximum(z, 0.0) + jnp.log(1.0 + jnp.exp(-jnp.abs(z))))
    if readable is not None:
        lk = jnp.where(readable, lk, 0.0)
    la = jnp.dot(lk.astype(BF16), tri, preferred_element_type=F32)
    return lk, la, jnp.sum(lk, axis=1, keepdims=True)


def _sb_finish(z, lk, log_after, readable):
    w = jnp.exp(z + lk + log_after)
    if readable is not None:
        w = jnp.where(readable, w, 0.0)
    return w


def _attn_prompt_body(q_ref, k_ref, v_ref, tri_ref, bias_ref, o_ref, acc_ref, z_ref, w_ref, *, scale, hd):
    qi = pl.program_id(2)
    tq = q_ref.shape[1]
    nh = q_ref.shape[2] // hd
    tri = tri_ref[...]
    lanes = [slice(hh * hd, (hh + 1) * hd) for hh in range(nh)]

    def key_rows(kb):
        return pl.ds(pl.multiple_of(kb * tq, tq), tq)

    def put_logits(kb):
        rows = key_rows(kb)
        for hh in range(nh):
            z_ref[hh] = lax.dot_general(q_ref[0, :, lanes[hh]], k_ref[0, rows, lanes[hh]],
                                        (((1,), (1,)), ((), ())), preferred_element_type=F32)

    def add_values(kb):
        rows = key_rows(kb)
        for hh in range(nh):
            acc_ref[hh] += jnp.dot(w_ref[hh], v_ref[0, rows, lanes[hh]], preferred_element_type=F32)

    def put_weights(zs, readable, carries):
        out = []
        for hh in range(nh):
            w, carry = _sb_weights(zs[hh] * scale + bias_ref[0, hh], readable, carries[hh], tri)
            w_ref[hh] = w.astype(w_ref.dtype)
            out.append(carry)
        return tuple(out)

    acc_ref[...] = jnp.zeros_like(acc_ref)
    row = lax.broadcasted_iota(jnp.int32, (tq, tq), 0)
    col = lax.broadcasted_iota(jnp.int32, (tq, tq), 1)
    put_logits(qi)
    zs = [z_ref[hh] for hh in range(nh)]
    put_logits(jnp.maximum(qi - 1, 0))
    carries = put_weights(zs, col < row, [jnp.zeros((tq, 1), F32)] * nh)

    def step(i, carries):
        kb = qi - 1 - i
        zs = [z_ref[hh] for hh in range(nh)]
        add_values(kb + 1)
        put_logits(jnp.maximum(kb - 1, 0))
        return put_weights(zs, None, carries)

    lax.fori_loop(0, qi, step, carries)
    add_values(0)
    for hh in range(nh):
        o_ref[0, :, lanes[hh]] = acc_ref[hh].astype(o_ref.dtype)


def _tri(tk):
    return jnp.asarray(np.arange(tk)[:, None] > np.arange(tk)[None, :], dtype=BF16)


def _attn_prompt(q16, k16, v16, sb_bias):
    b, t, d = q16.shape
    hd = d // N_HEADS
    tq = ATT_BLOCK
    nh = ATT_HEADS
    bias = jnp.broadcast_to(sb_bias.astype(F32).reshape(N_HEADS // nh, nh, 1, 1), (N_HEADS // nh, nh, 1, tq))
    qspec = pl.BlockSpec((1, tq, nh * hd), lambda bi, h, qi: (bi, qi, h))
    kvspec = pl.BlockSpec((1, t, nh * hd), lambda bi, h, qi: (bi, 0, h))
    return pl.pallas_call(
        functools.partial(_attn_prompt_body, scale=hd ** -0.5, hd=hd),
        out_shape=jax.ShapeDtypeStruct((b, t, d), BF16),
        grid=(b, N_HEADS // nh, t // tq),
        in_specs=[qspec, kvspec, kvspec, _const_spec((tq, tq)),
                  pl.BlockSpec((1, nh, 1, tq), lambda bi, h, qi: (h, 0, 0, 0))],
        out_specs=qspec,
        scratch_shapes=[pltpu.VMEM((nh, tq, hd), F32), pltpu.VMEM((nh, tq, tq), F32),
                        pltpu.VMEM((nh, tq, tq), BF16)],
        compiler_params=_cparams(("parallel", "parallel", "arbitrary"), 32),
        name="attn_prompt",
    )(q16, k16, v16, _tri(tq), bias)


def _attn_sample_body(pt_ref, q_ref, kn_ref, vn_ref, *rest, scale, n_pp):
    del pt_ref
    kc_refs, vc_refs = rest[:n_pp], rest[n_pp:2 * n_pp]
    tri_ref, bias_ref, o_ref, carry_ref, acc_ref = rest[2 * n_pp:]
    p = pl.program_id(1)
    q16 = q_ref[0]
    rows, hd = q16.shape
    n_q = rows // N_HEADS
    lt = tri_ref.shape[1]
    tri = tri_ref[...]
    bias = bias_ref[...]
    row = lax.broadcasted_iota(jnp.int32, (rows, lt), 0)
    lane = lax.broadcasted_iota(jnp.int32, (rows, lt), 1)
    own_head = (lane % N_HEADS) == (row // n_q)

    def blocks(kvs, readable):
        zs = [lax.dot_general(q16, k2, (((1,), (1,)), ((), ())), preferred_element_type=F32) for k2, _ in kvs]
        carry = carry_ref[...]
        tiles = []
        for z in zs:
            row_tiles = [None] * (z.shape[1] // lt)
            for t in reversed(range(len(row_tiles))):
                zt = z[:, t * lt:(t + 1) * lt] * scale + bias
                lk, la, total = _sb_log_keep(zt, readable, tri)
                row_tiles[t] = (zt, lk, la + carry)
                carry = carry + total
            tiles.append(row_tiles)
        carry_ref[...] = carry
        acc = acc_ref[...]
        for row_tiles, (_, v2) in zip(tiles, kvs):
            w = jnp.concatenate([_sb_finish(zt, lk, la, readable).astype(BF16) for zt, lk, la in row_tiles], axis=1)
            acc = acc + jnp.dot(w, v2, preferred_element_type=F32)
        acc_ref[...] = acc

    @pl.when(p == 0)
    def _():
        carry_ref[...] = jnp.zeros_like(carry_ref)
        acc_ref[...] = jnp.zeros_like(acc_ref)
        blocks([(kn_ref[0], vn_ref[0])], own_head & ((lane // N_HEADS) < (row % n_q)))

    def page_rows(ref):
        return ref[0].reshape(ref.shape[1] * ref.shape[2], hd).astype(BF16)

    blocks([(page_rows(kc_refs[j]), page_rows(vc_refs[j])) for j in range(n_pp)], own_head)

    @pl.when(p == pl.num_programs(1) - 1)
    def _():
        o_ref[0] = acc_ref[...]


def _attn_sample(q16, k_new, v_new, cache_k, cache_v, page_table, sb_bias):
    nb, n_new, d = q16.shape
    n_pages = page_table.shape[1]
    page, hd = cache_k.shape[1], cache_k.shape[3]
    n_pp = ATT_PAGES
    lt = 128
    rows = N_HEADS * n_new
    assert n_pages % n_pp == 0 and (N_HEADS * n_new) <= lt and lt % N_HEADS == 0

    def head_major(x, dtype):
        return x.reshape(nb, n_new, N_HEADS, hd).transpose(0, 2, 1, 3).reshape(nb, rows, hd).astype(dtype)

    def lane_rows(x):
        return jnp.pad(x.reshape(nb, n_new * N_HEADS, hd), ((0, 0), (0, lt - n_new * N_HEADS), (0, 0))).astype(BF16)

    bias = jnp.broadcast_to(jnp.repeat(sb_bias.astype(F32), n_new)[:, None], (rows, lt))
    row_spec = pl.BlockSpec((1, rows, hd), lambda b, p, pt: (b, 0, 0))
    new_spec = pl.BlockSpec((1, lt, hd), lambda b, p, pt: (b, 0, 0))

    def page_spec(j):
        return pl.BlockSpec((1, page, N_HEADS, hd),
                            lambda b, p, pt: (pt[b, n_pages - 1 - (p * n_pp + j)], 0, 0, 0))

    grid_spec = pltpu.PrefetchScalarGridSpec(
        num_scalar_prefetch=1,
        grid=(nb, n_pages // n_pp),
        in_specs=[row_spec, new_spec, new_spec] + [page_spec(j) for j in range(n_pp)] * 2
        + [pl.BlockSpec((lt, lt), lambda b, p, pt: (0, 0)), pl.BlockSpec((rows, lt), lambda b, p, pt: (0, 0))],
        out_specs=row_spec,
        scratch_shapes=[pltpu.VMEM((rows, 1), F32), pltpu.VMEM((rows, hd), F32)],
    )
    out = pl.pallas_call(
        functools.partial(_attn_sample_body, scale=hd ** -0.5, n_pp=n_pp),
        out_shape=jax.ShapeDtypeStruct((nb, rows, hd), F32),
        grid_spec=grid_spec,
        compiler_params=_cparams(("parallel", "arbitrary"), 40),
        name="attn_sample",
    )(page_table, head_major(q16, BF16), lane_rows(k_new), lane_rows(v_new),
      *([cache_k] * n_pp), *([cache_v] * n_pp), _tri(lt), bias)
    return out.reshape(nb, N_HEADS, n_new, hd).transpose(0, 2, 1, 3).reshape(nb, n_new, d)


def _spatial_gate_chunks(gv, w_ref, bs_ref):
    r, d = gv.shape
    gc = d // GM_GROUPS
    row = lax.broadcasted_iota(jnp.int32, (CHUNK, CHUNK), 0)
    col = lax.broadcasted_iota(jnp.int32, (CHUNK, CHUNK), 1)
    gv16 = gv.astype(BF16)
    cols = []
    for g in range(GM_GROUPS):
        w = jnp.where(col <= row, w_ref[g], 0.0).astype(BF16)
        parts = [jnp.dot(w, gv16[c * CHUNK:(c + 1) * CHUNK, g * gc:(g + 1) * gc], preferred_element_type=F32)
                 + bs_ref[:, g:g + 1] for c in range(r // CHUNK)]
        cols.append(jnp.concatenate(parts, axis=0))
    return jnp.concatenate(cols, axis=1)


def _spatial_gate_partial(gv3, wc_ref, bs_ref):
    tb, tt, d = gv3.shape
    gc = d // GM_GROUPS
    t_idx = lax.broadcasted_iota(jnp.int32, (tt, 1), 0)
    cols = []
    for g in range(GM_GROUPS):
        gvg = gv3[:, :, g * gc:(g + 1) * gc]
        mixed = jnp.broadcast_to(bs_ref[:, g:g + 1][None], (tb, tt, gc))
        for s in range(tt):
            wcol = jnp.where(t_idx >= s, wc_ref[g, s], 0.0)
            mixed = mixed + wcol[None] * gvg[:, s:s + 1, :]
        cols.append(mixed)
    return jnp.concatenate(cols, axis=2).reshape(tb * tt, d)


def _mix_body(u_ref, gv_ref, ga_ref, gb_ref, at_ref, x_ref, gate1_ref, shift2_ref, scale2_ref, g2_ref,
              wsp_ref, bs_ref, wba_ref, wbb_ref, wout_ref, x1_ref, h2_ref):
    tb, tt, d = x_ref.shape
    r = tb * tt
    if tt % CHUNK == 0:
        mixed = _spatial_gate_chunks(gv_ref[...].reshape(r, d), wsp_ref, bs_ref)
    else:
        mixed = _spatial_gate_partial(gv_ref[...], wsp_ref, bs_ref)
    out_a = (u_ref[...].reshape(r, d) * mixed).astype(BF16)
    out_b = at_ref[...].reshape(r, d).astype(BF16)
    merged = (ga_ref[...].reshape(r, d) * jnp.dot(out_a, wba_ref[...], preferred_element_type=F32)
              + gb_ref[...].reshape(r, d) * jnp.dot(out_b, wbb_ref[...], preferred_element_type=F32))
    mix = jnp.dot(merged.astype(BF16), wout_ref[...], preferred_element_type=F32).reshape(tb, tt, d)
    x1 = x_ref[...] + gate1_ref[...] * mix
    x1_ref[...] = x1
    h2_ref[...] = _rms_mod(x1, g2_ref[...], scale2_ref[...], shift2_ref[...]).astype(h2_ref.dtype)


def _mix(u, gv, ga, gb, attn, x, gate1, shift2, scale2, g2, w_spatial, b_spatial, wba16, wbb16, wout16, tb, tt):
    nb, t, d = x.shape
    tok = pl.BlockSpec((tb, tt, d), lambda i, j: (i, j, 0))
    mod = pl.BlockSpec((tb, 1, d), lambda i, j: (i, 0, 0))
    if tt % CHUNK == 0:
        wsp = w_spatial
        bs = b_spatial.T
    else:
        wsp = jnp.swapaxes(w_spatial[:, :tt, :tt], 1, 2)[..., None]
        bs = b_spatial.T[:tt]
    return pl.pallas_call(
        _mix_body,
        out_shape=[jax.ShapeDtypeStruct((nb, t, d), F32), jax.ShapeDtypeStruct((nb, t, d), BF16)],
        grid=(nb // tb, t // tt),
        in_specs=[tok] * 6 + [mod] * 3 + [_const_spec((1, d)), _const_spec(wsp.shape), _const_spec(bs.shape),
                                          _const_spec(wba16.shape), _const_spec(wbb16.shape),
                                          _const_spec(wout16.shape)],
        out_specs=[tok, tok],
        compiler_params=_cparams(("parallel", "parallel"), 48),
        name="mix",
    )(u, gv, ga, gb, attn, x, gate1, shift2, scale2, g2.reshape(1, d), wsp, bs, wba16, wbb16, wout16)


def _scores_body(h2_ref, wq_ref, sk_ref, st_ref, h2t_ref):
    h2 = h2_ref[...]
    nsk = sk_ref.shape[2]
    q16 = jnp.dot(h2, wq_ref[...], preferred_element_type=F32).astype(BF16)
    for hp in range(st_ref.shape[0]):
        st_ref[hp] = lax.dot_general(sk_ref[hp % 2], q16[:, hp * nsk:(hp + 1) * nsk],
                                     (((1,), (1,)), ((), ())), preferred_element_type=F32)
    h2t_ref[...] = h2.astype(F32).T.astype(h2t_ref.dtype)


def _scores(h2, wq16, sk16):
    n, d = h2.shape
    n_keys, dsk = sk16.shape[1], sk16.shape[2]
    nhp = wq16.shape[1] // dsk
    r = ROW_TILE
    return pl.pallas_call(
        _scores_body,
        out_shape=[jax.ShapeDtypeStruct((nhp, n_keys, n), F32), jax.ShapeDtypeStruct((d, n), BF16)],
        grid=(n // r,),
        in_specs=[pl.BlockSpec((r, d), lambda i: (i, 0)), _const_spec(wq16.shape), _const_spec(sk16.shape)],
        out_specs=[pl.BlockSpec((nhp, n_keys, r), lambda i: (0, 0, i)), pl.BlockSpec((d, r), lambda i: (0, i))],
        compiler_params=_cparams(("parallel",), 32),
        name="peer_scores",
    )(h2, wq16, sk16)


def _top16(x, iota, by_index):
    n = float(x.shape[0])
    rank = jnp.full(x.shape, float(TOPK), F32)
    vals = []
    for r in range(TOPK):
        m = jnp.max(x, axis=0, keepdims=True)
        sel = x == m
        if by_index:
            sel = iota == jnp.min(jnp.where(sel, iota, n), axis=0, keepdims=True)
        rank = jnp.where(sel, float(r), rank)
        x = jnp.where(sel, NEG_INF, x)
        vals.append(m)
    taken = jnp.sum(jnp.where(rank < float(TOPK), 1.0, 0.0), axis=0, keepdims=True)
    return jnp.concatenate(vals, axis=0), rank, taken


def _cand_layout():
    groups = [(0, list(range(16)))] + [(a, list(range(8))) for a in range(1, 8)]
    fidx = []
    for a, bs in groups:
        fidx += [a * TOPK + b if (a + 1) * (b + 1) <= TOPK else -1 for b in bs]
    fidx += [a * TOPK for a in range(8, 16)]
    return np.asarray(fidx, np.float32)


def _topk_body(st_ref, fidx_ref, a_ref, c0_ref, b_ref, r1_ref):
    n_keys, tok = st_ref.shape[1], st_ref.shape[2]
    iota = lax.broadcasted_iota(jnp.int32, (n_keys, tok), 0).astype(F32)
    fidx = fidx_ref[...]
    big = float(TOPK * TOPK)

    def select(h, by_index):
        s0 = st_ref[2 * h]
        s1 = st_ref[2 * h + 1]
        t0, rank0, n0 = _top16(s0, iota, by_index)
        t1, rank1, n1 = _top16(s1, iota, by_index)
        rows = [t0[0:1] + t1]
        rows += [t0[a:a + 1] + t1[0:8] for a in range(1, 8)]
        rows += [t0[8:16] + t1[0:1]]
        cand = jnp.where(fidx >= 0, jnp.concatenate(rows, axis=0), NEG_INF)
        taken = jnp.zeros(cand.shape, F32)
        zsum = jnp.zeros((1, tok), F32)
        top = t0[0:1] + t1[0:1]
        for _r in range(TOPK):
            m = jnp.max(cand, axis=0, keepdims=True)
            sel = cand == m
            if by_index:
                sel = fidx == jnp.min(jnp.where(sel, fidx, big), axis=0, keepdims=True)
            taken = jnp.where(sel, 1.0, taken)
            cand = jnp.where(sel, NEG_INF, cand)
            zsum = zsum + jnp.exp(m - top)
        counts = [jnp.sum(taken[0:16], axis=0, keepdims=True)]
        counts += [jnp.sum(taken[8 + 8 * a:16 + 8 * a], axis=0, keepdims=True) for a in range(1, 8)]
        counts.append(taken[72:80])
        cnt = jnp.concatenate(counts, axis=0)
        c0 = jnp.zeros((n_keys, tok), F32)
        for a in range(TOPK):
            c0 = jnp.where(rank0 == float(a), cnt[a:a + 1], c0)
        a_ref[h] = jnp.exp(s0 - t0[0:1]) / zsum
        c0_ref[h] = c0
        b_ref[h] = jnp.exp(s1 - t1[0:1]).astype(b_ref.dtype)
        r1_ref[h] = rank1.astype(r1_ref.dtype)
        excess = jnp.maximum(jnp.maximum(n0, n1), jnp.sum(cnt, axis=0, keepdims=True)) - float(TOPK)
        return jnp.max(excess)

    def head(h, _):
        tied = select(h, False) > 0.0

        @pl.when(tied)
        def _():
            select(h, True)

        return 0

    lax.fori_loop(0, a_ref.shape[0], head, 0)


def _topk(st):
    nhp, n_keys, n = st.shape
    nh = nhp // 2
    tok = TOPK_TOK
    fidx = jnp.asarray(np.broadcast_to(_cand_layout()[:, None], (80, tok)))
    ospec = pl.BlockSpec((nh, n_keys, tok), lambda i: (0, 0, i))
    return pl.pallas_call(
        _topk_body,
        out_shape=[jax.ShapeDtypeStruct((nh, n_keys, n), dt) for dt in (F32, F32, BF16, BF16)],
        grid=(n // tok,),
        in_specs=[pl.BlockSpec((nhp, n_keys, tok), lambda i: (0, 0, i)), _const_spec((80, tok))],
        out_specs=[ospec] * 4,
        compiler_params=_cparams(("parallel",), 32),
        name="peer_topk",
    )(st, fidx)


def _peer_gate_rows(act_ref, coef_ref, a_ref, c0_ref, b_ref, r1_ref, first, count):
    nh, n_grp, sub, tok = b_ref.shape
    n_keys = n_grp * sub
    for lt in range(tok // LANES):
        lanes = slice(lt * LANES, (lt + 1) * LANES)
        gates = [None] * count
        for h in range(nh):
            b = b_ref[h, :, :, lanes]
            r1 = r1_ref[h, :, :, lanes]
            for k in range(count):
                i = first + k
                cnt = jnp.broadcast_to(c0_ref[h, i:i + 1, lanes], (sub, LANES)).astype(BF16)[None]
                amp = jnp.broadcast_to(a_ref[h, i:i + 1, lanes], (sub, LANES)).astype(BF16)[None]
                term = jnp.where(r1 < cnt, b, jnp.zeros_like(b)) * amp
                gates[k] = term if gates[k] is None else gates[k] + term
        for k in range(count):
            rows = slice((first + k) * n_keys, (first + k + 1) * n_keys)
            coef_ref[rows, lanes] = gates[k].reshape(n_keys, LANES) * _gelu(act_ref[rows, lanes]).astype(BF16)


def _peer_body(h2t_ref, a_ref, c0_ref, b_ref, r1_ref, down_ref, upt_ref, x1_ref, gate2_ref, gf_ref,
               y_ref, acc_ref, act_ref, coef_ref, bp_ref, r1p_ref, *, final_norm):
    c = pl.program_id(1)
    ic = act_ref.shape[0] // (b_ref.shape[1] * b_ref.shape[2])

    @pl.when(c == 0)
    def _():
        acc_ref[...] = jnp.zeros_like(acc_ref)
        bp_ref[...] = b_ref[...]
        r1p_ref[...] = r1_ref[...]

    piece = act_ref.shape[0] // PEER_ACT_PIECES
    for r in range(PEER_ACT_PIECES):
        rows = slice(r * piece, (r + 1) * piece)
        act_ref[rows, :] = jnp.dot(down_ref[rows, :], h2t_ref[...], preferred_element_type=F32)
    for first in range(0, ic, PEER_ROW_GROUP):
        _peer_gate_rows(act_ref, coef_ref, a_ref, c0_ref, bp_ref, r1p_ref, first, PEER_ROW_GROUP)
    acc_ref[...] += jnp.dot(upt_ref[0], coef_ref[...], preferred_element_type=F32)

    @pl.when(c == pl.num_programs(1) - 1)
    def _():
        tb, tt, d = x1_ref.shape
        y = x1_ref[...] + gate2_ref[...] * acc_ref[...].T.reshape(tb, tt, d)
        if final_norm:
            y = (y * lax.rsqrt(jnp.mean(y * y, axis=-1, keepdims=True) + EPS)) * gf_ref[...]
        y_ref[...] = y


def _peer(h2t, a, c0, b16, r116, down16, upt16, x1, gate2, g_final, tb, tt, final_norm):
    nb, t, d = x1.shape
    nh, n_keys, n = b16.shape
    tok = tb * tt
    sub = BF16_SUBLANES
    packed = lambda x: x.reshape(nh, n_keys // sub, sub, n)
    ne = down16.shape[0]
    ech = PEER_ECHUNK
    ic = ech // n_keys
    per_tok = t // tt
    lane = pl.BlockSpec((nh, n_keys // sub, sub, tok), lambda i, c: (0, 0, 0, i))
    side = pl.BlockSpec((nh, ic, tok), lambda i, c: (0, c, i))
    tokspec = pl.BlockSpec((tb, tt, d), lambda i, c: (i // per_tok, i % per_tok, 0))
    return pl.pallas_call(
        functools.partial(_peer_body, final_norm=final_norm),
        out_shape=jax.ShapeDtypeStruct((nb, t, d), F32),
        grid=(n // tok, ne // ech),
        in_specs=[pl.BlockSpec((d, tok), lambda i, c: (0, i)), side, side, lane, lane,
                  pl.BlockSpec((ech, d), lambda i, c: (c, 0)),
                  pl.BlockSpec((1, d, ech), lambda i, c: (c, 0, 0)),
                  tokspec,
                  pl.BlockSpec((tb, 1, d), lambda i, c: (i // per_tok, 0, 0)),
                  pl.BlockSpec((1, d), lambda i, c: (0, 0))],
        out_specs=tokspec,
        scratch_shapes=[pltpu.VMEM((d, tok), F32), pltpu.VMEM((ech, tok), F32), pltpu.VMEM((ech, tok), BF16),
                        pltpu.VMEM((nh, n_keys // sub, sub, tok), BF16),
                        pltpu.VMEM((nh, n_keys // sub, sub, tok), BF16)],
        compiler_params=_cparams(("parallel", "arbitrary"), 48),
        name="peer",
    )(h2t, a, c0, packed(b16), packed(r116), down16, upt16, x1, gate2, g_final.reshape(1, d))


def _transpose_body(x_ref, o_ref):
    o_ref[0] = x_ref[...].T.astype(o_ref.dtype)


def _transpose_chunks_bf16(x, blk):
    r, c = x.shape
    cb = min(c, 1024)
    return pl.pallas_call(
        _transpose_body,
        out_shape=jax.ShapeDtypeStruct((r // blk, c, blk), BF16),
        grid=(r // blk, c // cb),
        in_specs=[pl.BlockSpec((blk, cb), lambda i, j: (i, j))],
        out_specs=pl.BlockSpec((1, cb, blk), lambda i, j: (i, j, 0)),
        compiler_params=_cparams(("parallel", "parallel"), 32),
        name="transpose_cast",
    )(x)


def _row_tiles(nb, t):
    if t % ROW_TILE == 0:
        return 1, ROW_TILE
    assert ROW_TILE % t == 0 and t % 8 == 0 and nb % (ROW_TILE // t) == 0, (nb, t)
    return ROW_TILE // t, t


def _layer(x, mod, weights, attend, final_norm, g_final):
    (g1, w_in16, g_v, b_v, w_spatial, b_spatial, wba16, wbb16, wout16, g2, wq16, sk16, down16, upt16) = weights
    nb, t, d = x.shape
    tb, tt = _row_tiles(nb, t)
    shift1, scale1, gate1, shift2, scale2, gate2 = (mod[:, i:i + 1, :] for i in range(N_MOD))
    q16, k, k16, v, v16, u, gv, ga, gb = _inproj(x, shift1, scale1, g1, w_in16, g_v, b_v, tb, tt)
    attn = attend(q16, k, k16, v, v16)
    x1, h2 = _mix(u, gv, ga, gb, attn, x, gate1, shift2, scale2, g2, w_spatial, b_spatial,
                  wba16, wbb16, wout16, tb, tt)
    st, h2t = _scores(h2.reshape(nb * t, d), wq16, sk16)
    a, c0, b, r1 = _topk(st)
    ptb, ptt = (1, PEER_TOK) if t % PEER_TOK == 0 else (PEER_TOK // t, t)
    y = _peer(h2t, a, c0, b, r1, down16, upt16, x1, gate2, g_final, ptb, ptt, final_norm)
    return y, k, v, gv


def kernel(x_prompt, x_sample, cache_k, cache_v, page_table, c_prompt, c_sample, w_ada, b_ada, g_norm1, w_in,
           sb_bias, g_v, b_v, w_spatial, b_spatial, w_branch_a, w_branch_b, w_out, g_norm2, w_query, sub_keys,
           expert_down, expert_up, g_final):
    depth = w_ada.shape[0]
    n_prompt, n_sample = c_prompt.shape[0], c_sample.shape[0]
    d = x_prompt.shape[-1]
    hd = d // N_HEADS
    c_all = jnp.concatenate([c_prompt, c_sample], axis=0)
    n_c = c_all.shape[0]
    c_all = jnp.pad(c_all, ((0, (-n_c) % 16), (0, 0)))
    xp, xs = x_prompt, x_sample
    rows = [[] for _ in range(5)]
    for l in range(depth):
        mod = _adaln(c_all, w_ada[l], b_ada[l]).reshape(-1, N_MOD, d)
        weights = (g_norm1[l], w_in[l].astype(BF16), g_v[l], b_v[l], w_spatial[l], b_spatial[l],
                   w_branch_a[l].astype(BF16), w_branch_b[l].astype(BF16), w_out[l].astype(BF16), g_norm2[l],
                   w_query[l].astype(BF16), sub_keys[l].astype(BF16), expert_down[l].astype(BF16),
                   _transpose_chunks_bf16(expert_up[l], PEER_ECHUNK))
        last = l == depth - 1
        xp, kp, vp, _ = _layer(
            xp, mod[:n_prompt], weights,
            lambda q16, k, k16, v, v16: _attn_prompt(q16, k16, v16, sb_bias[l]), last, g_final)
        xs, ks, vs, gs = _layer(
            xs, mod[n_prompt:n_c], weights,
            lambda q16, k, k16, v, v16: _attn_sample(q16, k, v, cache_k[l], cache_v[l], page_table, sb_bias[l]),
            last, g_final)
        for lst, val in zip(rows, (kp, vp, ks, vs, gs)):
            lst.append(val.reshape(val.shape[0], val.shape[1], N_HEADS, hd))
    if depth == 0:
        raise ValueError("depth must be positive")
    return (xp, xs) + tuple(jnp.stack(r) for r in rows)
```
